```python
import jax, jax.numpy as jnp
from jax import lax
import numpy as np

D_MODEL = 1024
BATCH = 16
SEQ = 2048
DEPTH = 2
DEC_BATCH = 128
DEC_SEQ = 4
PAST_LEN = 8192
PAGE_SIZE = 128

N_HEADS = 8
NOPE_DIM = 64
ROPE_DIM = 32
V_DIM = 64
Q_LORA = 256
KV_LORA = 128
MLA_WIDTH = N_HEADS * V_DIM
GM_GROUPS = 8
GM_HEAD_DIM = 64
GM_WIDTH = GM_GROUPS * GM_HEAD_DIM
CHUNK = 128
IN_WIDTH = Q_LORA + KV_LORA + ROPE_DIM + 2 * GM_WIDTH
D_FF = 4 * D_MODEL
ROPE_THETA = 10000.0
EPS = 1e-6
Q_BLOCK = 128
ATTN_SCALE = (NOPE_DIM + ROPE_DIM) ** -0.5
NEG_INF = -1e30

kernel_name = "hybrid_mla_chunk_gmlp_decode_step"


def rmsnorm(x, g):
    xf = x.astype(jnp.float32)
    y = xf * lax.rsqrt(jnp.mean(xf * xf, axis=-1, keepdims=True) + EPS)
    return (y * g.astype(jnp.float32)).astype(x.dtype)


def rope(x, pos):
    half = ROPE_DIM // 2
    freqs = ROPE_THETA ** (-jnp.arange(half, dtype=jnp.float32) / half)
    ang = pos.astype(jnp.float32)[:, None] * freqs[None, :]
    cos = jnp.cos(ang)[None, :, None, :]
    sin = jnp.sin(ang)[None, :, None, :]
    x1 = x[..., :half].astype(jnp.float32)
    x2 = x[..., half:].astype(jnp.float32)
    out = jnp.concatenate([x1 * cos - x2 * sin, x2 * cos + x1 * sin], axis=-1)
    return out.astype(x.dtype)


def project(h, pos, w_in, q_norm, w_uq, kv_norm, w_uk, v_norm):
    z = jnp.einsum('btd,de->bte', h, w_in)
    s1 = Q_LORA
    s2 = s1 + KV_LORA
    s3 = s2 + ROPE_DIM
    s4 = s3 + GM_WIDTH
    q_lat, c_kv, k_pe, u, v = jnp.split(z, [s1, s2, s3, s4], axis=-1)
    q = jnp.einsum('btr,rhe->bthe', rmsnorm(q_lat, q_norm), w_uq)
    q_nope, q_pe = q[..., :NOPE_DIM], q[..., NOPE_DIM:]
    q_pe = rope(q_pe, pos)
    q_abs = jnp.einsum('bthn,chn->bthc', q_nope, w_uk)
    c_kv = rmsnorm(c_kv, kv_norm)
    k_pe = rope(k_pe[:, :, None, :], pos)[:, :, 0, :]
    v = rmsnorm(v, v_norm)
    return q_abs, q_pe, c_kv, k_pe, u, v


def latent_attention(q_abs, q_pe, c_kv, k_pe, q_pos, k_pos, w_uv):
    s = (jnp.einsum('bqhc,bkc->bhqk', q_abs, c_kv, preferred_element_type=jnp.float32)
         + jnp.einsum('bqhr,bkr->bhqk', q_pe, k_pe, preferred_element_type=jnp.float32)) * ATTN_SCALE
    mask = k_pos[None, :] <= q_pos[:, None]
    s = jnp.where(mask[None, None], s, NEG_INF)
    p = jax.nn.softmax(s, axis=-1).astype(c_kv.dtype)
    o_lat = jnp.einsum('bhqk,bkc->bqhc', p, c_kv)
    o = jnp.einsum('bqhc,chv->bqhv', o_lat, w_uv)
    return o.reshape(o.shape[0], o.shape[1], MLA_WIDTH)


def prompt_attention(q_abs, q_pe, c_kv, k_pe, w_uv):
    B, T = q_abs.shape[0], q_abs.shape[1]
    nb = T // Q_BLOCK
    qa = q_abs.reshape(B, nb, Q_BLOCK, N_HEADS, KV_LORA).transpose(1, 0, 2, 3, 4)
    qp = q_pe.reshape(B, nb, Q_BLOCK, N_HEADS, ROPE_DIM).transpose(1, 0, 2, 3, 4)
    k_pos = jnp.arange(T, dtype=jnp.int32)

    def one_block(args):
        qa_b, qp_b, i = args
        q_pos = i * Q_BLOCK + jnp.arange(Q_BLOCK, dtype=jnp.int32)
        return latent_attention(qa_b, qp_b, c_kv, k_pe, q_pos, k_pos, w_uv)

    o = lax.map(one_block, (qa, qp, jnp.arange(nb, dtype=jnp.int32)))
    return o.transpose(1, 0, 2, 3).reshape(B, T, MLA_WIDTH)


def chunk_gating(u, v, w_s, b_s):
    B, T = u.shape[0], u.shape[1]
    n_chunks = -(-T // CHUNK)
    pad = n_chunks * CHUNK - T
    vp = jnp.pad(v, ((0, 0), (0, pad), (0, 0))).reshape(B, n_chunks, CHUNK, GM_GROUPS, GM_HEAD_DIM)
    w = w_s * jnp.tril(jnp.ones((CHUNK, CHUNK), dtype=w_s.dtype))
    mixed = jnp.einsum('gij,bnjgd->bnigd', w, vp) + b_s.T[None, None, :, :, None]
    mixed = mixed.reshape(B, n_chunks * CHUNK, GM_WIDTH)[:, :T]
    return u * mixed


def trunk_layer(x, pos, attend, nm_pre, nm_post, w_in, q_norm, w_uq, kv_norm, w_uk, w_uv,
                v_norm, w_s, b_s, w_out, nf_pre, nf_post, w_up, w_down):
    h = rmsnorm(x, nm_pre)
    q_abs, q_pe, c_kv, k_pe, u, v = project(h, pos, w_in, q_norm, w_uq, kv_norm, w_uk, v_norm)
    a_out = attend(q_abs, q_pe, c_kv, k_pe, w_uv)
    g_out = chunk_gating(u, v, w_s, b_s)
    mix = jnp.einsum('bte,ed->btd', jnp.concatenate([a_out, g_out], axis=-1), w_out)
    x = x + rmsnorm(mix, nm_post)
    hf = jnp.einsum('btd,df->btf', rmsnorm(x, nf_pre), w_up)
    f = jnp.einsum('btf,fd->btd', jnp.square(jax.nn.relu(hf)), w_down)
    x = x + rmsnorm(f, nf_post)
    return x, c_kv, k_pe, v


def setup_inputs(seed: int = 0) -> dict:
    key = jax.random.key(seed)
    ks = jax.random.split(key, 24)
    n_pages = PAST_LEN // PAGE_SIZE
    n_phys = (DEC_BATCH * n_pages * 5) // 4
    f32 = jnp.float32

    def nrm(k, shape, scale=1.0):
        return jax.random.normal(k, shape, f32) * scale

    def gain(k, shape):
        return 1.0 + 0.1 * jax.random.normal(k, shape, f32)

    page_table = jax.random.permutation(ks[4], n_phys)[:DEC_BATCH * n_pages]
    page_table = page_table.reshape(DEC_BATCH, n_pages).astype(jnp.int32)
    return {
        "x_prompt": nrm(ks[0], (BATCH, SEQ, D_MODEL)),
        "x_sample": nrm(ks[1], (DEC_BATCH, DEC_SEQ, D_MODEL)),
        "cache_kv_latent": nrm(ks[2], (DEPTH, n_phys, PAGE_SIZE, KV_LORA)),
        "cache_k_rope": nrm(ks[3], (DEPTH, n_phys, PAGE_SIZE, ROPE_DIM)),
        "page_table": page_table,
        "norm_mix_pre": gain(ks[5], (DEPTH, D_MODEL)),
        "norm_mix_post": gain(ks[6], (DEPTH, D_MODEL)),
        "w_in": nrm(ks[7], (DEPTH, D_MODEL, IN_WIDTH), D_MODEL ** -0.5),
        "q_norm": gain(ks[8], (DEPTH, Q_LORA)),
        "w_uq": nrm(ks[9], (DEPTH, Q_LORA, N_HEADS, NOPE_DIM + ROPE_DIM), Q_LORA ** -0.5),
        "kv_norm": gain(ks[10], (DEPTH, KV_LORA)),
        "w_uk": nrm(ks[11], (DEPTH, KV_LORA, N_HEADS, NOPE_DIM), KV_LORA ** -0.5),
        "w_uv": nrm(ks[12], (DEPTH, KV_LORA, N_HEADS, V_DIM), KV_LORA ** -0.5),
        "v_norm": gain(ks[13], (DEPTH, GM_WIDTH)),
        "w_spatial": nrm(ks[14], (DEPTH, GM_GROUPS, CHUNK, CHUNK), CHUNK ** -0.5),
        "b_spatial": gain(ks[15], (DEPTH, GM_GROUPS, CHUNK)),
        "w_out": nrm(ks[16], (DEPTH, MLA_WIDTH + GM_WIDTH, D_MODEL), (MLA_WIDTH + GM_WIDTH) ** -0.5),
        "norm_ffn_pre": gain(ks[17], (DEPTH, D_MODEL)),
        "norm_ffn_post": gain(ks[18], (DEPTH, D_MODEL)),
        "w_ff_up": nrm(ks[19], (DEPTH, D_MODEL, D_FF), D_MODEL ** -0.5),
        "w_ff_down": nrm(ks[20], (DEPTH, D_FF, D_MODEL), D_FF ** -0.5),
    }


def reference(x_prompt, x_sample, cache_kv_latent, cache_k_rope, page_table,
              norm_mix_pre, norm_mix_post, w_in, q_norm, w_uq, kv_norm, w_uk, w_uv,
              v_norm, w_spatial, b_spatial, w_out, norm_ffn_pre, norm_ffn_post,
              w_ff_up, w_ff_down):
    B, T = x_prompt.shape[0], x_prompt.shape[1]
    DB, DT = x_sample.shape[0], x_sample.shape[1]
    past_len = page_table.shape[1] * PAGE_SIZE
    pos_prompt = jnp.arange(T, dtype=jnp.int32)
    pos_sample = past_len + jnp.arange(DT, dtype=jnp.int32)

    xp, xs = x_prompt, x_sample
    ckv_p, kpe_p, ckv_s, kpe_s, v_s = [], [], [], [], []
    for l in range(DEPTH):
        weights = (norm_mix_pre[l], norm_mix_post[l], w_in[l], q_norm[l], w_uq[l], kv_norm[l],
                   w_uk[l], w_uv[l], v_norm[l], w_spatial[l], b_spatial[l], w_out[l],
                   norm_ffn_pre[l], norm_ffn_post[l], w_ff_up[l], w_ff_down[l])

        xp, c_p, k_p, _ = trunk_layer(xp, pos_prompt, prompt_attention, *weights)

        past_c = cache_kv_latent[l][page_table].reshape(DB, past_len, KV_LORA)
        past_k = cache_k_rope[l][page_table].reshape(DB, past_len, ROPE_DIM)
        k_pos = jnp.arange(past_len + DT, dtype=jnp.int32)

        def sample_attention(q_abs, q_pe, c_kv, k_pe, w_uv_l, past_c=past_c, past_k=past_k, k_pos=k_pos):
            keys_c = jnp.concatenate([past_c, c_kv], axis=1)
            keys_k = jnp.concatenate([past_k, k_pe], axis=1)
            return latent_attention(q_abs, q_pe, keys_c, keys_k, pos_sample, k_pos, w_uv_l)

        xs, c_s, k_s, vrows = trunk_layer(xs, pos_sample, sample_attention, *weights)

        ckv_p.append(c_p.reshape(B, T // PAGE_SIZE, PAGE_SIZE, KV_LORA))
        kpe_p.append(k_p.reshape(B, T // PAGE_SIZE, PAGE_SIZE, ROPE_DIM))
        ckv_s.append(c_s)
        kpe_s.append(k_s)
        v_s.append(vrows)

    new_kv_latent_prompt = jnp.stack(ckv_p, axis=0)
    new_k_rope_prompt = jnp.stack(kpe_p, axis=0)
    new_kv_latent_sample = jnp.stack(ckv_s, axis=0)
    new_k_rope_sample = jnp.stack(kpe_s, axis=0)
    new_chunk_v_sample = jnp.stack(v_s, axis=0)
    return (xp, xs, new_kv_latent_prompt, new_k_rope_prompt, new_kv_latent_sample, new_k_rope_sample, new_chunk_v_sample)
```

```python
import functools

import jax
import jax.numpy as jnp
from jax import lax
from jax.experimental import pallas as pl
from jax.experimental.pallas import tpu as pltpu

N_HEADS = 8
NOPE_DIM = 64
ROPE_DIM = 32
V_DIM = 64
Q_LORA = 256
KV_LORA = 128
GM_GROUPS = 8
GM_HEAD_DIM = 64
GM_WIDTH = GM_GROUPS * GM_HEAD_DIM
CHUNK = 128
ROPE_THETA = 10000.0
EPS = 1e-6
ATTN_SCALE = (NOPE_DIM + ROPE_DIM) ** -0.5
NEG_INF = -1e30

LANES = 128
KCAT = KV_LORA + ROPE_DIM
QABS_W = N_HEADS * KV_LORA
QPE_W = N_HEADS * ROPE_DIM

C_QLAT = 0
C_CKV = Q_LORA
C_U = C_CKV + KV_LORA
C_V = C_U + GM_WIDTH
C_KPE = C_V + GM_WIDTH
C_KPE_SW = C_KPE + LANES
IN_W = C_KPE_SW + LANES

VMEM_LIMIT = 56 * 1024 * 1024

F32 = jnp.float32
BF16 = jnp.bfloat16


def _rms(x, g):
    ms = jnp.mean(x * x, axis=-1, keepdims=True)
    return x * lax.rsqrt(ms + EPS) * g


def _dot(a, b):
    return jnp.dot(a, b, preferred_element_type=F32)


def _dot_nt(a, b):
    return lax.dot_general(a, b, (((1,), (1,)), ((), ())), preferred_element_type=F32)


def _prep_kernel(wqn_ref, wuk_ref, wuv_ref, wouta_ref, wabs_ref, wo_ref):
    for h in range(N_HEADS):
        wabs = _dot(wqn_ref[0, h], wuk_ref[0, h]) * ATTN_SCALE
        wabs_ref[0, :, h * KV_LORA:(h + 1) * KV_LORA] = wabs.astype(BF16)
        wo = _dot(wuv_ref[0, h], wouta_ref[0, h])
        wo_ref[0, h * KV_LORA:(h + 1) * KV_LORA, :] = wo.astype(BF16)


def _prep(w_uq, w_uk, w_uv, w_out):
    depth, _, _, _ = w_uq.shape
    d_model = w_out.shape[-1]
    wqn = jnp.transpose(w_uq[..., :NOPE_DIM], (0, 2, 1, 3))
    wuk = jnp.transpose(w_uk, (0, 2, 3, 1))
    wuv = jnp.transpose(w_uv, (0, 2, 1, 3))
    wouta = w_out[:, :N_HEADS * V_DIM].reshape(depth, N_HEADS, V_DIM, d_model)
    blk = lambda shp: pl.BlockSpec((1,) + shp, lambda l: (l,) + (0,) * len(shp))
    return pl.pallas_call(
        _prep_kernel,
        grid=(depth,),
        in_specs=[blk((N_HEADS, Q_LORA, NOPE_DIM)), blk((N_HEADS, NOPE_DIM, KV_LORA)),
                  blk((N_HEADS, KV_LORA, V_DIM)), blk((N_HEADS, V_DIM, d_model))],
        out_specs=[blk((Q_LORA, QABS_W)), blk((QABS_W, d_model))],
        out_shape=[jax.ShapeDtypeStruct((depth, Q_LORA, QABS_W), BF16),
                   jax.ShapeDtypeStruct((depth, QABS_W, d_model), BF16)],
        name="prep_fold",
    )(wqn, wuk, wuv, wouta)


def _proj_kernel(x_ref, nm_ref, win_ref, qn_ref, kvn_ref, vnw_ref, wq_ref, cos_ref, sin_ref,
                 gp_ref, gb_ref,
                 qabs_ref, qpe_ref, kcat_ref, ckv_ref, kpe_ref, gout_ref, *vn_out, period):
    tm = x_ref.shape[0]
    h = _rms(x_ref[...], nm_ref[...]).astype(BF16)
    z = _dot(h, win_ref[...])

    ql = _rms(z[:, C_QLAT:C_QLAT + Q_LORA], qn_ref[...]).astype(BF16)
    q = _dot(ql, wq_ref[...])
    cos = cos_ref[...]
    sin = sin_ref[...]
    qabs_ref[...] = q[:, :QABS_W].astype(BF16)
    qpe = q[:, QABS_W:QABS_W + QPE_W] * cos + q[:, QABS_W + QPE_W:] * sin
    qpe_ref[...] = qpe * ATTN_SCALE

    ckv = _rms(z[:, C_CKV:C_CKV + KV_LORA], kvn_ref[...])
    ckv_ref[...] = ckv
    kpe = (z[:, C_KPE:C_KPE + ROPE_DIM] * cos[:, :ROPE_DIM]
           + z[:, C_KPE_SW:C_KPE_SW + ROPE_DIM] * sin[:, :ROPE_DIM])
    kpe_ref[...] = kpe
    kcat_ref[:, :KV_LORA] = ckv.astype(BF16)
    kcat_ref[:, KV_LORA:] = kpe.astype(BF16)

    vn = _rms(z[:, C_V:C_V + GM_WIDTH], vnw_ref[...])
    if vn_out:
        vn_out[0][...] = vn
    row = lax.broadcasted_iota(jnp.int32, (CHUNK, 2 * CHUNK), 0)
    col = lax.broadcasted_iota(jnp.int32, (CHUNK, 2 * CHUNK), 1) & (CHUNK - 1)
    mask = col <= row
    if period < CHUNK:
        sh = period.bit_length() - 1
        mask = mask & ((row >> sh) == (col >> sh))
    left = lax.broadcasted_iota(jnp.int32, (CHUNK, LANES), 1) < GM_HEAD_DIM
    for k in range(GM_GROUPS // 2):
        gpk = jnp.where(mask, gp_ref[k], 0.0).astype(BF16)
        cs = slice(k * LANES, (k + 1) * LANES)
        for c in range(tm // CHUNK):
            rs = slice(c * CHUNK, (c + 1) * CHUNK)
            vcol = vn[rs, cs]
            rhs = jnp.concatenate([jnp.where(left, vcol, 0.0), jnp.where(left, 0.0, vcol)],
                                  axis=0).astype(BF16)
            mixed = _dot(gpk, rhs) + gb_ref[:, cs]
            gout_ref[rs, cs] = (z[rs, C_U + k * LANES:C_U + (k + 1) * LANES] * mixed).astype(BF16)


def _proj(x, w, cos, sin, gp, gb, *, period, emit_v, tm):
    n, d = x.shape
    n_pos_tiles = cos.shape[0] // tm
    const = lambda a: pl.BlockSpec(a.shape, lambda i: (0,) * a.ndim)
    rows = lambda wdt: pl.BlockSpec((tm, wdt), lambda i: (i, 0))
    pos = pl.BlockSpec((tm, QPE_W), lambda i: (i % n_pos_tiles, 0))
    out_w = [(QABS_W, BF16), (QPE_W, F32), (KCAT, BF16), (KV_LORA, F32), (ROPE_DIM, F32),
             (GM_WIDTH, BF16)]
    if emit_v:
        out_w.append((GM_WIDTH, F32))
    ins = [x, w["nm_pre"], w["w_in"], w["q_norm"], w["kv_norm"], w["v_norm"], w["w_q"]]
    return pl.pallas_call(
        functools.partial(_proj_kernel, period=period),
        grid=(n // tm,),
        in_specs=[rows(d)] + [const(a) for a in ins[1:]] + [pos, pos, const(gp), const(gb)],
        out_specs=[rows(wd) for wd, _ in out_w],
        out_shape=[jax.ShapeDtypeStruct((n, wd), dt) for wd, dt in out_w],
        compiler_params=pltpu.CompilerParams(dimension_semantics=("parallel",),
                                             vmem_limit_bytes=VMEM_LIMIT),
        name="proj",
    )(*ins, cos, sin, gp, gb)


def _attn_prompt_kernel(qabs_ref, qpe_ref, k_ref, o_ref, q_scr, *, blk):
    rows_all = N_HEADS * blk
    qb = pl.program_id(1)
    for h in range(N_HEADS):
        q_scr[h * blk:(h + 1) * blk, :KV_LORA] = qabs_ref[:, h * KV_LORA:(h + 1) * KV_LORA]
        q_scr[h * blk:(h + 1) * blk, KV_LORA:] = (
            qpe_ref[:, h * ROPE_DIM:(h + 1) * ROPE_DIM].astype(BF16))
    q = q_scr[...]

    def scores(kb):
        kblk = k_ref[pl.ds(pl.multiple_of(kb * blk, blk), blk), :]
        return _dot_nt(q, kblk), kblk[:, :KV_LORA]

    def update(s, v, carry):
        m, l, acc = carry
        m_new = jnp.maximum(m, jnp.max(s, axis=-1, keepdims=True))
        alpha = jnp.exp(m - m_new)
        p = jnp.exp(s - m_new)
        l = alpha * l + jnp.sum(p, axis=-1, keepdims=True)
        acc = alpha * acc + _dot(p.astype(BF16), v)
        return m_new, l, acc

    def body(kb, carry):
        s, v = scores(kb)
        return update(s, v, carry)

    init = (jnp.full((rows_all, 1), NEG_INF, F32), jnp.zeros((rows_all, 1), F32),
            jnp.zeros((rows_all, KV_LORA), F32))
    carry = lax.fori_loop(0, qb, body, init)
    s, v = scores(qb)
    rpos = lax.broadcasted_iota(jnp.int32, (rows_all, blk), 0) & (blk - 1)
    cpos = lax.broadcasted_iota(jnp.int32, (rows_all, blk), 1)
    s = jnp.where(cpos <= rpos, s, NEG_INF)
    _, l, acc = update(s, v, carry)
    o = (acc / l).astype(BF16)
    for h in range(N_HEADS):
        o_ref[:, h * KV_LORA:(h + 1) * KV_LORA] = o[h * blk:(h + 1) * blk, :]


def _attn_prompt(qabs, qpe, kcat, *, batch, seq, blk):
    nq = seq // blk
    return pl.pallas_call(
        functools.partial(_attn_prompt_kernel, blk=blk),
        grid=(batch, nq),
        in_specs=[pl.BlockSpec((blk, QABS_W), lambda b, i: (b * nq + i, 0)),
                  pl.BlockSpec((blk, QPE_W), lambda b, i: (b * nq + i, 0)),
                  pl.BlockSpec((seq, KCAT), lambda b, i: (b, 0))],
        out_specs=pl.BlockSpec((blk, QABS_W), lambda b, i: (b * nq + i, 0)),
        out_shape=jax.ShapeDtypeStruct((batch * seq, QABS_W), BF16),
        scratch_shapes=[pltpu.VMEM((N_HEADS * blk, KCAT), BF16)],
        compiler_params=pltpu.CompilerParams(dimension_semantics=("parallel", "parallel"),
                                             vmem_limit_bytes=VMEM_LIMIT),
        name="attn_prompt",
    )(qabs, qpe, kcat)


def _attn_sample_kernel(pt_ref, q_ref, knew_ref, ckv_hbm, kr_hbm, o_ref, kvbuf, krbuf, sems,
                        *, layer, n_pages, page, dec_seq):
    b = pl.program_id(0)
    nb = pl.num_programs(0)
    slot = b % 2

    def page_copies(pg, p, sl):
        return (pltpu.make_async_copy(ckv_hbm.at[layer, pg], kvbuf.at[sl, pl.ds(p * page, page)],
                                      sems.at[0, sl]),
                pltpu.make_async_copy(kr_hbm.at[layer, pg], krbuf.at[sl, p], sems.at[1, sl]))

    def start_all(bb, sl):
        for p in range(n_pages):
            for cp in page_copies(pt_ref[bb, p], p, sl):
                cp.start()

    @pl.when(b == 0)
    def _():
        start_all(0, 0)

    @pl.when(b + 1 < nb)
    def _():
        start_all(b + 1, 1 - slot)

    for p in range(n_pages):
        for cp in page_copies(0, p, slot):
            cp.wait()

    q = q_ref[0]
    qa = q[:, :KV_LORA]
    qp = q[:, KV_LORA:]
    kv = kvbuf[slot].astype(BF16)
    kr = krbuf[slot].reshape(n_pages * page, ROPE_DIM).astype(BF16)
    s_past = _dot_nt(qa, kv) + _dot_nt(qp, kr)
    knew = knew_ref[0]
    s_new = _dot_nt(q, knew)
    rows = q.shape[0]
    tpos = lax.broadcasted_iota(jnp.int32, (rows, LANES), 0) & (dec_seq - 1)
    jpos = lax.broadcasted_iota(jnp.int32, (rows, LANES), 1)
    s_new = jnp.where(jpos <= tpos, s_new, NEG_INF)
    m = jnp.maximum(jnp.max(s_past, axis=-1, keepdims=True), jnp.max(s_new, axis=-1, keepdims=True))
    p_past = jnp.exp(s_past - m)
    p_new = jnp.exp(s_new - m)
    l = jnp.sum(p_past, axis=-1, keepdims=True) + jnp.sum(p_new, axis=-1, keepdims=True)
    acc = _dot(p_past.astype(BF16), kv) + _dot(p_new.astype(BF16), knew[:, :KV_LORA])
    o_ref[0] = (acc / l).astype(BF16)


def _attn_sample(page_table, qs, knew, cache_kv, cache_kr, *, layer):
    db, rows, _ = qs.shape
    n_pages = page_table.shape[1]
    page = cache_kv.shape[2]
    dec_seq = rows // N_HEADS
    grid_spec = pltpu.PrefetchScalarGridSpec(
        num_scalar_prefetch=1,
        grid=(db,),
        in_specs=[pl.BlockSpec((1, rows, KCAT), lambda b, pt: (b, 0, 0)),
                  pl.BlockSpec((1, LANES, KCAT), lambda b, pt: (b, 0, 0)),
                  pl.BlockSpec(memory_space=pl.ANY),
                  pl.BlockSpec(memory_space=pl.ANY)],
        out_specs=pl.BlockSpec((1, rows, KV_LORA), lambda b, pt: (b, 0, 0)),
        scratch_shapes=[pltpu.VMEM((2, n_pages * page, KV_LORA), F32),
                        pltpu.VMEM((2, n_pages, page, ROPE_DIM), F32),
                        pltpu.SemaphoreType.DMA((2, 2))],
    )
    return pl.pallas_call(
        functools.partial(_attn_sample_kernel, layer=layer, n_pages=n_pages, page=page,
                          dec_seq=dec_seq),
        grid_spec=grid_spec,
        out_shape=jax.ShapeDtypeStruct((db, rows, KV_LORA), BF16),
        compiler_params=pltpu.CompilerParams(dimension_semantics=("arbitrary",),
                                             vmem_limit_bytes=VMEM_LIMIT),
        name="attn_sample",
    )(page_table, qs, knew, cache_kv, cache_kr)


def _post_kernel(x_ref, ol_ref, g_ref, wo_ref, wg_ref, nmp_ref, nfp_ref, nfo_ref, wup_ref, wdn_ref,
                 y_ref, *, ff_chunk):
    mix = _dot(ol_ref[...], wo_ref[...]) + _dot(g_ref[...], wg_ref[...])
    x1 = x_ref[...] + _rms(mix, nmp_ref[...])
    h2 = _rms(x1, nfp_ref[...]).astype(BF16)
    d_ff = wup_ref.shape[1]
    f = None
    for c in range(d_ff // ff_chunk):
        cs = slice(c * ff_chunk, (c + 1) * ff_chunk)
        hf = _dot(h2, wup_ref[:, cs])
        act = jnp.square(jnp.maximum(hf, 0.0)).astype(BF16)
        part = _dot(act, wdn_ref[cs, :])
        f = part if f is None else f + part
    y_ref[...] = x1 + _rms(f, nfo_ref[...])


def _post(x, ol, gout, w, *, tm, ff_chunk):
    n, d = x.shape
    const = lambda a: pl.BlockSpec(a.shape, lambda i: (0,) * a.ndim, pipeline_mode=pl.Buffered(1))
    rows = lambda wdt: pl.BlockSpec((tm, wdt), lambda i: (i, 0))
    ws = [w["w_o"], w["w_out_g"], w["nm_post"], w["nf_pre"], w["nf_post"], w["w_up"], w["w_down"]]
    return pl.pallas_call(
        functools.partial(_post_kernel, ff_chunk=ff_chunk),
        grid=(n // tm,),
        in_specs=[rows(d), rows(QABS_W), rows(GM_WIDTH)] + [const(a) for a in ws],
        out_specs=rows(d),
        out_shape=jax.ShapeDtypeStruct((n, d), F32),
        compiler_params=pltpu.CompilerParams(dimension_semantics=("parallel",),
                                             vmem_limit_bytes=VMEM_LIMIT),
        name="post",
    )(x, ol, gout, *ws)


def _rope_tables(pos):
    half = ROPE_DIM // 2
    freqs = ROPE_THETA ** (-jnp.arange(half, dtype=F32) / half)
    ang = pos.astype(F32)[:, None] * freqs[None, :]
    tile = lambda t: jnp.tile(t, (1, 2 * N_HEADS))
    return tile(jnp.cos(ang)), tile(jnp.sin(ang))


def _swap_halves(w):
    half = w.shape[-1] // 2
    return jnp.concatenate([-w[..., half:], w[..., :half]], axis=-1)


def _pair_groups(g):
    n = g.shape[0]
    return g.reshape(n // 2, 2, CHUNK, CHUNK).transpose(0, 2, 1, 3).reshape(n // 2, CHUNK, 2 * CHUNK)


def kernel(x_prompt, x_sample, cache_kv_latent, cache_k_rope, page_table, norm_mix_pre, norm_mix_post,
           w_in, q_norm, w_uq, kv_norm, w_uk, w_uv, v_norm, w_spatial, b_spatial, w_out, norm_ffn_pre,
           norm_ffn_post, w_ff_up, w_ff_down):
    batch, seq, d_model = x_prompt.shape
    dec_batch, dec_seq, _ = x_sample.shape
    depth = w_in.shape[0]
    n_pages, page = page_table.shape[1], cache_kv_latent.shape[2]
    past_len = n_pages * page
    tm = 512
    attn_blk = 256
    ff_chunk = 1024

    cos_p, sin_p = _rope_tables(jnp.arange(seq, dtype=jnp.int32))
    pos_s = past_len + (jnp.arange(dec_batch * dec_seq, dtype=jnp.int32) % dec_seq)
    cos_s, sin_s = _rope_tables(pos_s)

    w_abs, w_o = _prep(w_uq, w_uk, w_uv, w_out)

    xp = x_prompt.reshape(batch * seq, d_model)
    xs = x_sample.reshape(dec_batch * dec_seq, d_model)
    ckv_p, kpe_p, ckv_s, kpe_s, v_s = [], [], [], [], []
    row = lambda a: a.reshape(1, -1)
    for l in range(depth):
        wi = w_in[l]
        kpe_w = wi[:, C_U:C_U + ROPE_DIM]
        pad = jnp.zeros((d_model, LANES - ROPE_DIM), F32)
        w_in_r = jnp.concatenate(
            [wi[:, :Q_LORA + KV_LORA], wi[:, Q_LORA + KV_LORA + ROPE_DIM:], kpe_w, pad,
             _swap_halves(kpe_w), pad], axis=1).astype(BF16)
        wq_pe = w_uq[l][:, :, NOPE_DIM:]
        w_q = jnp.concatenate(
            [w_abs[l], wq_pe.reshape(Q_LORA, QPE_W).astype(BF16),
             _swap_halves(wq_pe).reshape(Q_LORA, QPE_W).astype(BF16)], axis=1)
        w = dict(nm_pre=row(norm_mix_pre[l]), w_in=w_in_r, q_norm=row(q_norm[l]),
                 kv_norm=row(kv_norm[l]), v_norm=row(v_norm[l]), w_q=w_q,
                 w_o=w_o[l], w_out_g=w_out[l][N_HEADS * V_DIM:].astype(BF16),
                 nm_post=row(norm_mix_post[l]), nf_pre=row(norm_ffn_pre[l]),
                 nf_post=row(norm_ffn_post[l]), w_up=w_ff_up[l].astype(BF16),
                 w_down=w_ff_down[l].astype(BF16))
        ws, bs = w_spatial[l], b_spatial[l]

        gp = _pair_groups(ws)
        gb = jnp.repeat(bs.T, GM_HEAD_DIM, axis=1)
        qabs, qpe, kcat, ckv, kpe, gout = _proj(xp, w, cos_p, sin_p, gp, gb, period=CHUNK,
                                                emit_v=False, tm=tm)
        ol = _attn_prompt(qabs, qpe, kcat, batch=batch, seq=seq, blk=attn_blk)
        xp = _post(xp, ol, gout, w, tm=tm, ff_chunk=ff_chunk)
        ckv_p.append(ckv.reshape(batch, seq // page, page, KV_LORA))
        kpe_p.append(kpe.reshape(batch, seq // page, page, ROPE_DIM))

        reps = CHUNK // dec_seq
        gp = _pair_groups(jnp.tile(ws[:, :dec_seq, :dec_seq], (1, reps, reps)))
        gb = jnp.tile(jnp.repeat(bs[:, :dec_seq].T, GM_HEAD_DIM, axis=1), (reps, 1))
        qabs, qpe, kcat, ckv, kpe, gout, vn = _proj(xs, w, cos_s, sin_s, gp, gb, period=dec_seq,
                                                    emit_v=True, tm=dec_batch * dec_seq)
        heads_to_rows = lambda a, wd: a.reshape(dec_batch, dec_seq, N_HEADS, wd).transpose(
            0, 2, 1, 3).reshape(dec_batch, N_HEADS * dec_seq, wd)
        qs = jnp.concatenate([heads_to_rows(qabs, KV_LORA), heads_to_rows(qpe.astype(BF16), ROPE_DIM)],
                             axis=-1)
        knew = jnp.pad(kcat.reshape(dec_batch, dec_seq, KCAT), ((0, 0), (0, LANES - dec_seq), (0, 0)))
        o = _attn_sample(page_table, qs, knew, cache_kv_latent, cache_k_rope, layer=l)
        ol = o.reshape(dec_batch, N_HEADS, dec_seq, KV_LORA).transpose(0, 2, 1, 3).reshape(
            dec_batch * dec_seq, QABS_W)
        xs = _post(xs, ol, gout, w, tm=dec_batch * dec_seq, ff_chunk=ff_chunk)
        ckv_s.append(ckv.reshape(dec_batch, dec_seq, KV_LORA))
        kpe_s.append(kpe.reshape(dec_batch, dec_seq, ROPE_DIM))
        v_s.append(vn.reshape(dec_batch, dec_seq, GM_WIDTH))

    return (xp.reshape(batch, seq, d_model), xs.reshape(dec_batch, dec_seq, d_model),
            jnp.stack(ckv_p), jnp.stack(kpe_p), jnp.stack(ckv_s), jnp.stack(kpe_s), jnp.stack(v_s))
```

```python
import functools

import jax
import jax.numpy as jnp
import numpy as np
from jax import lax
from jax.experimental import pallas as pl
from jax.experimental.pallas import tpu as pltpu

N_HEADS = 8
NOPE_DIM = 64
ROPE_DIM = 32
V_DIM = 64
Q_LORA = 256
KV_LORA = 128
GM_GROUPS = 8
GM_HEAD_DIM = 64
GM_WIDTH = GM_GROUPS * GM_HEAD_DIM
CHUNK = 128
ROPE_THETA = 10000.0
EPS = 1e-6
ATTN_SCALE = (NOPE_DIM + ROPE_DIM) ** -0.5
LOG2E = 1.4426950408889634
Q_SCALE = ATTN_SCALE * LOG2E
NEG_INF = -1e30

LANES = 128
KCAT = KV_LORA + ROPE_DIM
QABS_W = N_HEADS * KV_LORA
QPE_W = N_HEADS * ROPE_DIM

C_QLAT = 0
C_CKV = Q_LORA
C_U = C_CKV + KV_LORA
C_V = C_U + GM_WIDTH
C_KPE = C_V + GM_WIDTH
C_KPE_SW = C_KPE + LANES
IN_W = C_KPE_SW + LANES

VMEM_LIMIT = 56 * 1024 * 1024

F32 = jnp.float32
BF16 = jnp.bfloat16


def _rms(x, g):
    ms = jnp.mean(x * x, axis=-1, keepdims=True)
    return x * lax.rsqrt(ms + EPS) * g


def _dot(a, b):
    return jnp.dot(a, b, preferred_element_type=F32)


def _lane_tile(x, n):
    return jnp.concatenate([x] * n, axis=1)


def _dot_nt(a, b):
    return lax.dot_general(a, b, (((1,), (1,)), ((), ())), preferred_element_type=F32)


def _prep_kernel(wqn_ref, wuk_ref, wuv_ref, wouta_ref, wabs_ref, wo_ref):
    for h in range(N_HEADS):
        wabs = _dot(wqn_ref[0, h], wuk_ref[0, h]) * Q_SCALE
        wabs_ref[0, :, h * KV_LORA:(h + 1) * KV_LORA] = wabs.astype(BF16)
        wo = _dot(wuv_ref[0, h], wouta_ref[0, h])
        wo_ref[0, h * KV_LORA:(h + 1) * KV_LORA, :] = wo.astype(BF16)


def _prep(w_uq, w_uk, w_uv, w_out):
    depth, _, _, _ = w_uq.shape
    d_model = w_out.shape[-1]
    wqn = jnp.transpose(w_uq[..., :NOPE_DIM], (0, 2, 1, 3))
    wuk = jnp.transpose(w_uk, (0, 2, 3, 1))
    wuv = jnp.transpose(w_uv, (0, 2, 1, 3))
    wouta = w_out[:, :N_HEADS * V_DIM].reshape(depth, N_HEADS, V_DIM, d_model)
    blk = lambda shp: pl.BlockSpec((1,) + shp, lambda l: (l,) + (0,) * len(shp))
    return pl.pallas_call(
        _prep_kernel,
        grid=(depth,),
        in_specs=[blk((N_HEADS, Q_LORA, NOPE_DIM)), blk((N_HEADS, NOPE_DIM, KV_LORA)),
                  blk((N_HEADS, KV_LORA, V_DIM)), blk((N_HEADS, V_DIM, d_model))],
        out_specs=[blk((Q_LORA, QABS_W)), blk((QABS_W, d_model))],
        out_shape=[jax.ShapeDtypeStruct((depth, Q_LORA, QABS_W), BF16),
                   jax.ShapeDtypeStruct((depth, QABS_W, d_model), BF16)],
        name="prep_fold",
    )(wqn, wuk, wuv, wouta)


def _proj_kernel(x_ref, nm_ref, win_ref, qn_ref, kvn_ref, vnw_ref, wq_ref, cos_ref, sin_ref,
                 gp_ref, gb_ref,
                 qabs_ref, qpe_ref, kcat_ref, ckv_ref, kpe_ref, gout_ref, *vn_out, period):
    tm = x_ref.shape[0]
    h = _rms(x_ref[...], nm_ref[...]).astype(BF16)
    z = _dot(h, win_ref[...])

    ql = _rms(z[:, C_QLAT:C_QLAT + Q_LORA], qn_ref[...]).astype(BF16)
    q = _dot(ql, wq_ref[...])
    cos = cos_ref[...]
    sin = sin_ref[...]
    qabs_ref[...] = q[:, :QABS_W].astype(BF16)
    qpe = q[:, QABS_W:QABS_W + QPE_W] * cos + q[:, QABS_W + QPE_W:] * sin
    qpe_ref[...] = qpe * Q_SCALE

    ckv = _rms(z[:, C_CKV:C_CKV + KV_LORA], kvn_ref[...])
    ckv_ref[...] = ckv
    kpe = (z[:, C_KPE:C_KPE + ROPE_DIM] * cos[:, :ROPE_DIM]
           + z[:, C_KPE_SW:C_KPE_SW + ROPE_DIM] * sin[:, :ROPE_DIM])
    kpe_ref[...] = kpe
    kcat_ref[:, :KV_LORA] = ckv.astype(BF16)
    kcat_ref[:, KV_LORA:] = kpe.astype(BF16)

    vn = _rms(z[:, C_V:C_V + GM_WIDTH], vnw_ref[...])
    if vn_out:
        vn_out[0][...] = vn
    row = lax.broadcasted_iota(jnp.int32, (CHUNK, 2 * CHUNK), 0)
    col = lax.broadcasted_iota(jnp.int32, (CHUNK, 2 * CHUNK), 1) & (CHUNK - 1)
    mask = col <= row
    if period < CHUNK:
        sh = period.bit_length() - 1
        mask = mask & ((row >> sh) == (col >> sh))
    left = lax.broadcasted_iota(jnp.int32, (CHUNK, LANES), 1) < GM_HEAD_DIM
    for k in range(GM_GROUPS // 2):
        gpk = jnp.where(mask, gp_ref[k], 0.0).astype(BF16)
        cs = slice(k * LANES, (k + 1) * LANES)
        for c in range(tm // CHUNK):
            rs = slice(c * CHUNK, (c + 1) * CHUNK)
            vcol = vn[rs, cs]
            rhs = jnp.concatenate([jnp.where(left, vcol, 0.0), jnp.where(left, 0.0, vcol)],
                                  axis=0).astype(BF16)
            mixed = _dot(gpk, rhs) + gb_ref[:, cs]
            gout_ref[rs, cs] = (z[rs, C_U + k * LANES:C_U + (k + 1) * LANES] * mixed).astype(BF16)


def _proj(x, w, cos, sin, gp, gb, *, period, emit_v, tm):
    n, d = x.shape
    n_pos_tiles = cos.shape[0] // tm
    const = lambda a: pl.BlockSpec(a.shape, lambda i: (0,) * a.ndim)
    rows = lambda wdt: pl.BlockSpec((tm, wdt), lambda i: (i, 0))
    pos = pl.BlockSpec((tm, QPE_W), lambda i: (i % n_pos_tiles, 0))
    out_w = [(QABS_W, BF16), (QPE_W, F32), (KCAT, BF16), (KV_LORA, F32), (ROPE_DIM, F32),
             (GM_WIDTH, BF16)]
    if emit_v:
        out_w.append((GM_WIDTH, F32))
    ins = [x, w["nm_pre"], w["w_in"], w["q_norm"], w["kv_norm"], w["v_norm"], w["w_q"]]
    return pl.pallas_call(
        functools.partial(_proj_kernel, period=period),
        grid=(n // tm,),
        in_specs=[rows(d)] + [const(a) for a in ins[1:]] + [pos, pos, const(gp), const(gb)],
        out_specs=[rows(wd) for wd, _ in out_w],
        out_shape=[jax.ShapeDtypeStruct((n, wd), dt) for wd, dt in out_w],
        compiler_params=pltpu.CompilerParams(dimension_semantics=("parallel",),
                                             vmem_limit_bytes=VMEM_LIMIT),
        name="proj",
    )(*ins, cos, sin, gp, gb)


def _attn_prompt_kernel(qabs_ref, qpe_ref, k_ref, o_ref, q_scr, s_scr, p_scr, m_scr, a_scr, acc_scr,
                        *, blk, kblk, groups, rchunk):
    gheads = N_HEADS // groups
    grows = gheads * blk
    reps = kblk // LANES
    qb = pl.program_id(1)
    for h in range(N_HEADS):
        q_scr[h * blk:(h + 1) * blk, :KV_LORA] = qabs_ref[:, h * KV_LORA:(h + 1) * KV_LORA]
        q_scr[h * blk:(h + 1) * blk, KV_LORA:] = (
            qpe_ref[:, h * ROPE_DIM:(h + 1) * ROPE_DIM].astype(BF16))
    m_scr[...] = jnp.full(m_scr.shape, NEG_INF, F32)
    acc_scr[...] = jnp.zeros(acc_scr.shape, F32)
    n_full = (qb * blk) // kblk

    def max_rows(r0, kb, masked):
        rows = slice(r0, r0 + rchunk)
        s = s_scr[rows, :]
        if masked:
            qpos = qb * blk + ((r0 + lax.broadcasted_iota(jnp.int32, s.shape, 0)) & (blk - 1))
            kpos = kb * kblk + lax.broadcasted_iota(jnp.int32, s.shape, 1)
            s = jnp.where(kpos <= qpos, s, NEG_INF)
            s_scr[rows, :] = s
        m_old = m_scr[rows, :]
        m_new = jnp.maximum(m_old, jnp.max(s, axis=-1, keepdims=True))
        a_scr[rows, :] = jnp.exp2(m_old - m_new)
        m_scr[rows, :] = m_new

    def prob_rows(r0):
        rows = slice(r0, r0 + rchunk)
        p = jnp.exp2(s_scr[rows, :] - _lane_tile(m_scr[rows, :], reps))
        p_scr[rows, :] = p.astype(BF16)

    ones = jnp.ones((kblk, LANES), BF16)

    def key_block(kb, masked):
        kt = k_ref[pl.ds(pl.multiple_of(kb * kblk, kblk), kblk), :]
        vext = jnp.concatenate([kt[:, :KV_LORA], ones], axis=1)
        for g in range(groups):
            grp = slice(g * grows, (g + 1) * grows)
            s_scr[grp, :] = _dot_nt(q_scr[grp, :], kt)
        for g in range(groups):
            grp = slice(g * grows, (g + 1) * grows)
            for c in range(grows // rchunk):
                max_rows(g * grows + c * rchunk, kb, masked)
            for c in range(grows // rchunk):
                prob_rows(g * grows + c * rchunk)
            acc_scr[grp, :] = (_lane_tile(a_scr[grp, :], 2) * acc_scr[grp, :]
                               + _dot(p_scr[grp, :], vext))

    def body(kb, carry):
        key_block(kb, False)
        return carry

    lax.fori_loop(0, n_full, body, 0)
    key_block(n_full, True)
    o = (acc_scr[:, :KV_LORA] / acc_scr[:, KV_LORA:]).astype(BF16)
    for h in range(N_HEADS):
        o_ref[:, h * KV_LORA:(h + 1) * KV_LORA] = o[h * blk:(h + 1) * blk, :]


def _attn_prompt(qabs, qpe, kcat, *, batch, seq, blk, kblk, groups, rchunk):
    nq = seq // blk
    rows_all = N_HEADS * blk
    return pl.pallas_call(
        functools.partial(_attn_prompt_kernel, blk=blk, kblk=kblk, groups=groups, rchunk=rchunk),
        grid=(batch, nq),
        in_specs=[pl.BlockSpec((blk, QABS_W), lambda b, i: (b * nq + i, 0)),
                  pl.BlockSpec((blk, QPE_W), lambda b, i: (b * nq + i, 0)),
                  pl.BlockSpec((seq, KCAT), lambda b, i: (b, 0))],
        out_specs=pl.BlockSpec((blk, QABS_W), lambda b, i: (b * nq + i, 0)),
        out_shape=jax.ShapeDtypeStruct((batch * seq, QABS_W), BF16),
        scratch_shapes=[pltpu.VMEM((rows_all, KCAT), BF16),
                        pltpu.VMEM((rows_all, kblk), F32),
                        pltpu.VMEM((rows_all, kblk), BF16),
                        pltpu.VMEM((rows_all, LANES), F32),
                        pltpu.VMEM((rows_all, LANES), F32),
                        pltpu.VMEM((rows_all, KV_LORA + LANES), F32)],
        compiler_params=pltpu.CompilerParams(dimension_semantics=("parallel", "parallel"),
                                             vmem_limit_bytes=VMEM_LIMIT),
        name="attn_prompt",
    )(qabs, qpe, kcat)


def _attn_sample_kernel(pt_ref, q_ref, knew_ref, ckv_hbm, krt_hbm, o_ref, kvbuf, krbuf, sems,
                        *, layer, n_pages, page, dec_seq, nb):
    i = pl.program_id(0)
    nsteps = pl.num_programs(0)
    slot = i % 2

    def page_copies(pg, j, p, sl):
        cols = pl.ds(p * page, page)
        return (pltpu.make_async_copy(ckv_hbm.at[layer, pg], kvbuf.at[sl, j, cols], sems.at[0, sl]),
                pltpu.make_async_copy(krt_hbm.at[layer, pg], krbuf.at[sl, j, :, cols], sems.at[1, sl]))

    def start_all(step, sl):
        for j in range(nb):
            for p in range(n_pages):
                for cp in page_copies(pt_ref[step * nb + j, p], j, p, sl):
                    cp.start()

    @pl.when(i == 0)
    def _():
        start_all(0, 0)

    @pl.when(i + 1 < nsteps)
    def _():
        start_all(i + 1, 1 - slot)

    for j in range(nb):
        for p in range(n_pages):
            for cp in page_copies(0, j, p, slot):
                cp.wait()

    rows = q_ref.shape[1]
    tpos = lax.broadcasted_iota(jnp.int32, (rows, LANES), 0) & (dec_seq - 1)
    jpos = lax.broadcasted_iota(jnp.int32, (rows, LANES), 1)
    new_visible = jpos <= tpos
    for j in range(nb):
        q = q_ref[j]
        kv = kvbuf[slot, j].astype(BF16)
        krt = krbuf[slot, j].astype(BF16)
        s_past = _dot_nt(q[:, :KV_LORA], kv) + _dot(q[:, KV_LORA:], krt)
        knew = knew_ref[j]
        s_new = jnp.where(new_visible, _dot_nt(q, knew), NEG_INF)
        m = jnp.maximum(jnp.max(s_past, axis=-1, keepdims=True),
                        jnp.max(s_new, axis=-1, keepdims=True))
        p_past = jnp.exp2(s_past - m)
        p_new = jnp.exp2(s_new - m)
        l = jnp.sum(p_past, axis=-1, keepdims=True) + jnp.sum(p_new, axis=-1, keepdims=True)
        acc = _dot(p_past.astype(BF16), kv) + _dot(p_new.astype(BF16), knew[:, :KV_LORA])
        o_ref[j] = (acc / l).astype(BF16)


def _attn_sample(page_table, qs, knew, cache_kv, cache_krt, *, layer, nb):
    db, rows, _ = qs.shape
    n_pages = page_table.shape[1]
    page = cache_kv.shape[2]
    dec_seq = rows // N_HEADS
    grid_spec = pltpu.PrefetchScalarGridSpec(
        num_scalar_prefetch=1,
        grid=(db // nb,),
        in_specs=[pl.BlockSpec((nb, rows, KCAT), lambda i, pt: (i, 0, 0)),
                  pl.BlockSpec((nb, LANES, KCAT), lambda i, pt: (i, 0, 0)),
                  pl.BlockSpec(memory_space=pl.ANY),
                  pl.BlockSpec(memory_space=pl.ANY)],
        out_specs=pl.BlockSpec((nb, rows, KV_LORA), lambda i, pt: (i, 0, 0)),
        scratch_shapes=[pltpu.VMEM((2, nb, n_pages * page, KV_LORA), F32),
                        pltpu.VMEM((2, nb, ROPE_DIM, n_pages * page), F32),
                        pltpu.SemaphoreType.DMA((2, 2))],
    )
    return pl.pallas_call(
        functools.partial(_attn_sample_kernel, layer=layer, n_pages=n_pages, page=page,
                          dec_seq=dec_seq, nb=nb),
        grid_spec=grid_spec,
        out_shape=jax.ShapeDtypeStruct((db, rows, KV_LORA), BF16),
        compiler_params=pltpu.CompilerParams(dimension_semantics=("arbitrary",),
                                             vmem_limit_bytes=VMEM_LIMIT),
        name="attn_sample",
    )(page_table, qs, knew, cache_kv, cache_krt)


def _post_kernel(x_ref, ol_ref, g_ref, wo_ref, wg_ref, nmp_ref, nfp_ref, nfo_ref, wup_ref, wdn_ref,
                 y_ref, *, ff_chunk):
    mix = _dot(ol_ref[...], wo_ref[...]) + _dot(g_ref[...], wg_ref[...])
    x1 = x_ref[...] + _rms(mix, nmp_ref[...])
    h2 = _rms(x1, nfp_ref[...]).astype(BF16)
    d_ff = wup_ref.shape[1]
    f = None
    for c in range(d_ff // ff_chunk):
        cs = slice(c * ff_chunk, (c + 1) * ff_chunk)
        hf = _dot(h2, wup_ref[:, cs])
        act = jnp.square(jnp.maximum(hf, 0.0)).astype(BF16)
        part = _dot(act, wdn_ref[cs, :])
        f = part if f is None else f + part
    y_ref[...] = x1 + _rms(f, nfo_ref[...])


def _post(x, ol, gout, w, *, tm, ff_chunk):
    n, d = x.shape
    const = lambda a: pl.BlockSpec(a.shape, lambda i: (0,) * a.ndim, pipeline_mode=pl.Buffered(1))
    rows = lambda wdt: pl.BlockSpec((tm, wdt), lambda i: (i, 0))
    ws = [w["w_o"], w["w_out_g"], w["nm_post"], w["nf_pre"], w["nf_post"], w["w_up"], w["w_down"]]
    return pl.pallas_call(
        functools.partial(_post_kernel, ff_chunk=ff_chunk),
        grid=(n // tm,),
        in_specs=[rows(d), rows(QABS_W), rows(GM_WIDTH)] + [const(a) for a in ws],
        out_specs=rows(d),
        out_shape=jax.ShapeDtypeStruct((n, d), F32),
        compiler_params=pltpu.CompilerParams(dimension_semantics=("parallel",),
                                             vmem_limit_bytes=VMEM_LIMIT),
        name="post",
    )(x, ol, gout, *ws)


def _rope_tables(pos):
    half = ROPE_DIM // 2
    freqs = np.float32(ROPE_THETA) ** (-np.arange(half, dtype=np.float32) / np.float32(half))
    ang = pos.astype(np.float32)[:, None] * freqs[None, :]
    tile = lambda t: jnp.asarray(np.tile(t.astype(np.float32), (1, 2 * N_HEADS)))
    return tile(np.cos(ang)), tile(np.sin(ang))


def _swap_halves(w):
    half = w.shape[-1] // 2
    return jnp.concatenate([-w[..., half:], w[..., :half]], axis=-1)


def _pair_groups(g):
    n = g.shape[0]
    return g.reshape(n // 2, 2, CHUNK, CHUNK).transpose(0, 2, 1, 3).reshape(n // 2, CHUNK, 2 * CHUNK)


def kernel(x_prompt, x_sample, cache_kv_latent, cache_k_rope, page_table, norm_mix_pre, norm_mix_post,
           w_in, q_norm, w_uq, kv_norm, w_uk, w_uv, v_norm, w_spatial, b_spatial, w_out, norm_ffn_pre,
           norm_ffn_post, w_ff_up, w_ff_down):
    batch, seq, d_model = x_prompt.shape
    dec_batch, dec_seq, _ = x_sample.shape
    depth = w_in.shape[0]
    n_pages, page = page_table.shape[1], cache_kv_latent.shape[2]
    past_len = n_pages * page
    tm = 512
    attn_blk = 256
    attn_kblk = 512
    attn_groups = 8
    attn_rchunk = 32
    ff_chunk = 1024
    sample_nb = 2

    cos_p, sin_p = _rope_tables(np.arange(seq))
    cos_s, sin_s = _rope_tables(past_len + np.arange(dec_batch * dec_seq) % dec_seq)
    cache_krt = jnp.swapaxes(cache_k_rope, 2, 3)

    w_abs, w_o = _prep(w_uq, w_uk, w_uv, w_out)

    xp = x_prompt.reshape(batch * seq, d_model)
    xs = x_sample.reshape(dec_batch * dec_seq, d_model)
    ckv_p, kpe_p, ckv_s, kpe_s, v_s = [], [], [], [], []
    row = lambda a: a.reshape(1, -1)
    for l in range(depth):
        wi = w_in[l]
        kpe_w = wi[:, Q_LORA + KV_LORA:Q_LORA + KV_LORA + ROPE_DIM]
        pad = jnp.zeros((d_model, LANES - ROPE_DIM), F32)
        w_in_r = jnp.concatenate(
            [wi[:, :Q_LORA + KV_LORA], wi[:, Q_LORA + KV_LORA + ROPE_DIM:], kpe_w, pad,
             _swap_halves(kpe_w), pad], axis=1).astype(BF16)
        wq_pe = w_uq[l][:, :, NOPE_DIM:]
        w_q = jnp.concatenate(
            [w_abs[l], wq_pe.reshape(Q_LORA, QPE_W).astype(BF16),
             _swap_halves(wq_pe).reshape(Q_LORA, QPE_W).astype(BF16)], axis=1)
        w = dict(nm_pre=row(norm_mix_pre[l]), w_in=w_in_r, q_norm=row(q_norm[l]),
                 kv_norm=row(kv_norm[l]), v_norm=row(v_norm[l]), w_q=w_q,
                 w_o=w_o[l], w_out_g=w_out[l][N_HEADS * V_DIM:].astype(BF16),
                 nm_post=row(norm_mix_post[l]), nf_pre=row(norm_ffn_pre[l]),
                 nf_post=row(norm_ffn_post[l]), w_up=w_ff_up[l].astype(BF16),
                 w_down=w_ff_down[l].astype(BF16))
        ws, bs = w_spatial[l], b_spatial[l]

        gp = _pair_groups(ws)
        gb = jnp.repeat(bs.T, GM_HEAD_DIM, axis=1)
        qabs, qpe, kcat, ckv, kpe, gout = _proj(xp, w, cos_p, sin_p, gp, gb, period=CHUNK,
                                                emit_v=False, tm=tm)
        ol = _attn_prompt(qabs, qpe, kcat, batch=batch, seq=seq, blk=attn_blk, kblk=attn_kblk,
                          groups=attn_groups, rchunk=attn_rchunk)
        xp = _post(xp, ol, gout, w, tm=tm, ff_chunk=ff_chunk)
        ckv_p.append(ckv.reshape(batch, seq // page, page, KV_LORA))
        kpe_p.append(kpe.reshape(batch, seq // page, page, ROPE_DIM))

        reps = CHUNK // dec_seq
        gp = _pair_groups(jnp.tile(jnp.tile(ws[:, :dec_seq, :dec_seq], (1, 1, reps)), (1, reps, 1)))
        gb = jnp.tile(jnp.repeat(bs[:, :dec_seq].T, GM_HEAD_DIM, axis=1), (reps, 1))
        qabs, qpe, kcat, ckv, kpe, gout, vn = _proj(xs, w, cos_s, sin_s, gp, gb, period=dec_seq,
                                                    emit_v=True, tm=dec_batch * dec_seq)
        heads_to_rows = lambda a, wd: a.reshape(dec_batch, dec_seq, N_HEADS, wd).transpose(
            0, 2, 1, 3).reshape(dec_batch, N_HEADS * dec_seq, wd)
        qs = jnp.concatenate([heads_to_rows(qabs, KV_LORA), heads_to_rows(qpe.astype(BF16), ROPE_DIM)],
                             axis=-1)
        knew = jnp.pad(kcat.reshape(dec_batch, dec_seq, KCAT), ((0, 0), (0, LANES - dec_seq), (0, 0)))
        o = _attn_sample(page_table, qs, knew, cache_kv_latent, cache_krt, layer=l, nb=sample_nb)
        ol = o.reshape(dec_batch, N_HEADS, dec_seq, KV_LORA).transpose(0, 2, 1, 3).reshape(
            dec_batch * dec_seq, QABS_W)
        xs = _post(xs, ol, gout, w, tm=dec_batch * dec_seq, ff_chunk=ff_chunk)
        ckv_s.append(ckv.reshape(dec_batch, dec_seq, KV_LORA))
        kpe_s.append(kpe.reshape(dec_batch, dec_seq, ROPE_DIM))
        v_s.append(vn.reshape(dec_batch, dec_seq, GM_WIDTH))

    return (xp.reshape(batch, seq, d_model), xs.reshape(dec_batch, dec_seq, d_model),
            jnp.stack(ckv_p), jnp.stack(kpe_p), jnp.stack(ckv_s), jnp.stack(kpe_s), jnp.stack(v_s))
```

```python
import functools

import jax
import jax.numpy as jnp
import numpy as np
from jax import lax
from jax.experimental import pallas as pl
from jax.experimental.pallas import tpu as pltpu

N_HEADS = 8
NOPE_DIM = 64
ROPE_DIM = 32
V_DIM = 64
Q_LORA = 256
KV_LORA = 128
GM_GROUPS = 8
GM_HEAD_DIM = 64
GM_WIDTH = GM_GROUPS * GM_HEAD_DIM
CHUNK = 128
ROPE_THETA = 10000.0
EPS = 1e-6
ATTN_SCALE = (NOPE_DIM + ROPE_DIM) ** -0.5
LOG2E = 1.4426950408889634
Q_SCALE = ATTN_SCALE * LOG2E
NEG_INF = -1e30

LANES = 128
KCAT = KV_LORA + ROPE_DIM
QABS_W = N_HEADS * KV_LORA
QPE_W = N_HEADS * ROPE_DIM

C_QLAT = 0
C_CKV = Q_LORA
C_U = C_CKV + KV_LORA
C_V = C_U + GM_WIDTH
C_KPE = C_V + GM_WIDTH
C_KPE_SW = C_KPE + LANES
IN_W = C_KPE_SW + LANES

VMEM_LIMIT = 56 * 1024 * 1024

F32 = jnp.float32
BF16 = jnp.bfloat16


def _rms(x, g):
    ms = jnp.mean(x * x, axis=-1, keepdims=True)
    return x * lax.rsqrt(ms + EPS) * g


def _dot(a, b):
    return jnp.dot(a, b, preferred_element_type=F32)


def _lane_tile(x, n):
    return jnp.concatenate([x] * n, axis=1)


def _dot_nt(a, b):
    return lax.dot_general(a, b, (((1,), (1,)), ((), ())), preferred_element_type=F32)


def _prep_kernel(wqn_ref, wuk_ref, wuv_ref, wouta_ref, wabs_ref, wo_ref):
    for h in range(N_HEADS):
        wabs = _dot(wqn_ref[0, h], wuk_ref[0, h]) * Q_SCALE
        wabs_ref[0, :, h * KV_LORA:(h + 1) * KV_LORA] = wabs.astype(BF16)
        wo = _dot(wuv_ref[0, h], wouta_ref[0, h])
        wo_ref[0, h * KV_LORA:(h + 1) * KV_LORA, :] = wo.astype(BF16)


def _prep(w_uq, w_uk, w_uv, w_out):
    depth, _, _, _ = w_uq.shape
    d_model = w_out.shape[-1]
    wqn = jnp.transpose(w_uq[..., :NOPE_DIM], (0, 2, 1, 3))
    wuk = jnp.transpose(w_uk, (0, 2, 3, 1))
    wuv = jnp.transpose(w_uv, (0, 2, 1, 3))
    wouta = w_out[:, :N_HEADS * V_DIM].reshape(depth, N_HEADS, V_DIM, d_model)
    blk = lambda shp: pl.BlockSpec((1,) + shp, lambda l: (l,) + (0,) * len(shp))
    return pl.pallas_call(
        _prep_kernel,
        grid=(depth,),
        in_specs=[blk((N_HEADS, Q_LORA, NOPE_DIM)), blk((N_HEADS, NOPE_DIM, KV_LORA)),
                  blk((N_HEADS, KV_LORA, V_DIM)), blk((N_HEADS, V_DIM, d_model))],
        out_specs=[blk((Q_LORA, QABS_W)), blk((QABS_W, d_model))],
        out_shape=[jax.ShapeDtypeStruct((depth, Q_LORA, QABS_W), BF16),
                   jax.ShapeDtypeStruct((depth, QABS_W, d_model), BF16)],
        name="prep_fold",
    )(wqn, wuk, wuv, wouta)


def _proj_kernel(x_ref, nm_ref, win_ref, qn_ref, kvn_ref, vnw_ref, wq_ref, cos_ref, sin_ref,
                 gp_ref, gb_ref,
                 qabs_ref, qpe_ref, kcat_ref, ckv_ref, kpe_ref, gout_ref, *vn_out, period):
    tm = x_ref.shape[0]
    h = _rms(x_ref[...], nm_ref[...]).astype(BF16)
    z = _dot(h, win_ref[...])

    ql = _rms(z[:, C_QLAT:C_QLAT + Q_LORA], qn_ref[...]).astype(BF16)
    q = _dot(ql, wq_ref[...])
    cos = cos_ref[...]
    sin = sin_ref[...]
    qabs_ref[...] = q[:, :QABS_W].astype(BF16)
    qpe = q[:, QABS_W:QABS_W + QPE_W] * cos + q[:, QABS_W + QPE_W:] * sin
    qpe_ref[...] = qpe * Q_SCALE

    ckv = _rms(z[:, C_CKV:C_CKV + KV_LORA], kvn_ref[...])
    ckv_ref[...] = ckv
    kpe = (z[:, C_KPE:C_KPE + ROPE_DIM] * cos[:, :ROPE_DIM]
           + z[:, C_KPE_SW:C_KPE_SW + ROPE_DIM] * sin[:, :ROPE_DIM])
    kpe_ref[...] = kpe
    kcat_ref[:, :KV_LORA] = ckv.astype(BF16)
    kcat_ref[:, KV_LORA:] = kpe.astype(BF16)

    vn = _rms(z[:, C_V:C_V + GM_WIDTH], vnw_ref[...])
    if vn_out:
        vn_out[0][...] = vn
    row = lax.broadcasted_iota(jnp.int32, (CHUNK, 2 * CHUNK), 0)
    col = lax.broadcasted_iota(jnp.int32, (CHUNK, 2 * CHUNK), 1) & (CHUNK - 1)
    mask = col <= row
    if period < CHUNK:
        sh = period.bit_length() - 1
        mask = mask & ((row >> sh) == (col >> sh))
    left = lax.broadcasted_iota(jnp.int32, (CHUNK, LANES), 1) < GM_HEAD_DIM
    for k in range(GM_GROUPS // 2):
        gpk = jnp.where(mask, gp_ref[k], 0.0).astype(BF16)
        cs = slice(k * LANES, (k + 1) * LANES)
        for c in range(tm // CHUNK):
            rs = slice(c * CHUNK, (c + 1) * CHUNK)
            vcol = vn[rs, cs]
            rhs = jnp.concatenate([jnp.where(left, vcol, 0.0), jnp.where(left, 0.0, vcol)],
                                  axis=0).astype(BF16)
            mixed = _dot(gpk, rhs) + gb_ref[:, cs]
            gout_ref[rs, cs] = (z[rs, C_U + k * LANES:C_U + (k + 1) * LANES] * mixed).astype(BF16)


def _proj(x, w, cos, sin, gp, gb, *, period, emit_v, tm):
    n, d = x.shape
    n_pos_tiles = cos.shape[0] // tm
    const = lambda a: pl.BlockSpec(a.shape, lambda i: (0,) * a.ndim)
    rows = lambda wdt: pl.BlockSpec((tm, wdt), lambda i: (i, 0))
    pos = pl.BlockSpec((tm, QPE_W), lambda i: (i % n_pos_tiles, 0))
    out_w = [(QABS_W, BF16), (QPE_W, F32), (KCAT, BF16), (KV_LORA, F32), (ROPE_DIM, F32),
             (GM_WIDTH, BF16)]
    if emit_v:
        out_w.append((GM_WIDTH, F32))
    ins = [x, w["nm_pre"], w["w_in"], w["q_norm"], w["kv_norm"], w["v_norm"], w["w_q"]]
    return pl.pallas_call(
        functools.partial(_proj_kernel, period=period),
        grid=(n // tm,),
        in_specs=[rows(d)] + [const(a) for a in ins[1:]] + [pos, pos, const(gp), const(gb)],
        out_specs=[rows(wd) for wd, _ in out_w],
        out_shape=[jax.ShapeDtypeStruct((n, wd), dt) for wd, dt in out_w],
        compiler_params=pltpu.CompilerParams(dimension_semantics=("parallel",),
                                             vmem_limit_bytes=VMEM_LIMIT),
        name="proj",
    )(*ins, cos, sin, gp, gb)


def _attn_prompt_kernel(qabs_ref, qpe_ref, k_ref, o_ref, q_scr, s_scr, p_scr, m_scr, a_scr, acc_scr,
                        *, blk, kblk, groups, rchunk):
    gheads = N_HEADS // groups
    grows = gheads * blk
    qb = pl.program_id(1)
    for h in range(N_HEADS):
        q_scr[h * blk:(h + 1) * blk, :KV_LORA] = qabs_ref[:, h * KV_LORA:(h + 1) * KV_LORA]
        q_scr[h * blk:(h + 1) * blk, KV_LORA:] = (
            qpe_ref[:, h * ROPE_DIM:(h + 1) * ROPE_DIM].astype(BF16))
    m_scr[...] = jnp.full(m_scr.shape, NEG_INF, F32)
    acc_scr[...] = jnp.zeros(acc_scr.shape, F32)
    n_full = (qb * blk) // kblk

    def max_rows(r0, k0, width, masked):
        rows = slice(r0, r0 + rchunk)
        s = s_scr[rows, :width]
        if masked:
            qpos = qb * blk + ((r0 + lax.broadcasted_iota(jnp.int32, s.shape, 0)) & (blk - 1))
            kpos = k0 + lax.broadcasted_iota(jnp.int32, s.shape, 1)
            s = jnp.where(kpos <= qpos, s, NEG_INF)
            s_scr[rows, :width] = s
        m_old = m_scr[rows, :]
        m_new = jnp.maximum(m_old, jnp.max(s, axis=-1, keepdims=True))
        a_scr[rows, :] = jnp.exp2(m_old - m_new)
        m_scr[rows, :] = m_new

    def prob_rows(r0, width):
        rows = slice(r0, r0 + rchunk)
        p = jnp.exp2(s_scr[rows, :width] - _lane_tile(m_scr[rows, :], width // LANES))
        p_scr[rows, :width] = p.astype(BF16)

    def key_block(k0, width, masked):
        kt = k_ref[pl.ds(pl.multiple_of(k0, width), width), :]
        vext = jnp.concatenate([kt[:, :KV_LORA], jnp.ones((width, LANES), BF16)], axis=1)
        for g in range(groups):
            grp = slice(g * grows, (g + 1) * grows)
            s_scr[grp, :width] = _dot_nt(q_scr[grp, :], kt)
        for g in range(groups):
            grp = slice(g * grows, (g + 1) * grows)
            for c in range(grows // rchunk):
                max_rows(g * grows + c * rchunk, k0, width, masked)
            for c in range(grows // rchunk):
                prob_rows(g * grows + c * rchunk, width)
            acc_scr[grp, :] = (_lane_tile(a_scr[grp, :], 2) * acc_scr[grp, :]
                               + _dot(p_scr[grp, :width], vext))

    def body(kb, carry):
        key_block(kb * kblk, kblk, False)
        return carry

    lax.fori_loop(0, n_full, body, 0)
    k_tail = n_full * kblk
    aligned = k_tail == qb * blk

    @pl.when(aligned)
    def _():
        key_block(k_tail, blk, True)

    @pl.when(jnp.logical_not(aligned))
    def _():
        key_block(k_tail, kblk, True)

    o = (acc_scr[:, :KV_LORA] / acc_scr[:, KV_LORA:]).astype(BF16)
    for h in range(N_HEADS):
        o_ref[:, h * KV_LORA:(h + 1) * KV_LORA] = o[h * blk:(h + 1) * blk, :]


def _attn_prompt(qabs, qpe, kcat, *, batch, seq, blk, kblk, groups, rchunk):
    nq = seq // blk
    rows_all = N_HEADS * blk
    return pl.pallas_call(
        functools.partial(_attn_prompt_kernel, blk=blk, kblk=kblk, groups=groups, rchunk=rchunk),
        grid=(batch, nq),
        in_specs=[pl.BlockSpec((blk, QABS_W), lambda b, i: (b * nq + i, 0)),
                  pl.BlockSpec((blk, QPE_W), lambda b, i: (b * nq + i, 0)),
                  pl.BlockSpec((seq, KCAT), lambda b, i: (b, 0))],
        out_specs=pl.BlockSpec((blk, QABS_W), lambda b, i: (b * nq + i, 0)),
        out_shape=jax.ShapeDtypeStruct((batch * seq, QABS_W), BF16),
        scratch_shapes=[pltpu.VMEM((rows_all, KCAT), BF16),
                        pltpu.VMEM((rows_all, kblk), F32),
                        pltpu.VMEM((rows_all, kblk), BF16),
                        pltpu.VMEM((rows_all, LANES), F32),
                        pltpu.VMEM((rows_all, LANES), F32),
                        pltpu.VMEM((rows_all, KV_LORA + LANES), F32)],
        compiler_params=pltpu.CompilerParams(dimension_semantics=("parallel", "parallel"),
                                             vmem_limit_bytes=VMEM_LIMIT),
        name="attn_prompt",
    )(qabs, qpe, kcat)


def _attn_sample_kernel(pt_ref, q_ref, knew_ref, ckv_hbm, krt_hbm, o_ref, kvbuf, krbuf, sems,
                        *, layer, n_pages, page, dec_seq, nb):
    i = pl.program_id(0)
    nsteps = pl.num_programs(0)
    slot = i % 2

    def page_copies(pg, j, p, sl):
        cols = pl.ds(p * page, page)
        return (pltpu.make_async_copy(ckv_hbm.at[layer, pg], kvbuf.at[sl, j, cols], sems.at[0, sl]),
                pltpu.make_async_copy(krt_hbm.at[layer, pg], krbuf.at[sl, j, :, cols], sems.at[1, sl]))

    def start_all(step, sl):
        for j in range(nb):
            for p in range(n_pages):
                for cp in page_copies(pt_ref[step * nb + j, p], j, p, sl):
                    cp.start()

    @pl.when(i == 0)
    def _():
        start_all(0, 0)

    @pl.when(i + 1 < nsteps)
    def _():
        start_all(i + 1, 1 - slot)

    for j in range(nb):
        for p in range(n_pages):
            for cp in page_copies(0, j, p, slot):
                cp.wait()

    rows = q_ref.shape[1]
    tpos = lax.broadcasted_iota(jnp.int32, (rows, LANES), 0) & (dec_seq - 1)
    jpos = lax.broadcasted_iota(jnp.int32, (rows, LANES), 1)
    new_visible = jpos <= tpos
    for j in range(nb):
        q = q_ref[j]
        kv = kvbuf[slot, j].astype(BF16)
        krt = krbuf[slot, j].astype(BF16)
        s_past = _dot_nt(q[:, :KV_LORA], kv) + _dot(q[:, KV_LORA:], krt)
        knew = knew_ref[j]
        s_new = jnp.where(new_visible, _dot_nt(q, knew), NEG_INF)
        m = jnp.maximum(jnp.max(s_past, axis=-1, keepdims=True),
                        jnp.max(s_new, axis=-1, keepdims=True))
        p_past = jnp.exp2(s_past - m)
        p_new = jnp.exp2(s_new - m)
        l = jnp.sum(p_past, axis=-1, keepdims=True) + jnp.sum(p_new, axis=-1, keepdims=True)
        acc = _dot(p_past.astype(BF16), kv) + _dot(p_new.astype(BF16), knew[:, :KV_LORA])
        o_ref[j] = (acc / l).astype(BF16)


def _attn_sample(page_table, qs, knew, cache_kv, cache_krt, *, layer, nb):
    db, rows, _ = qs.shape
    n_pages = page_table.shape[1]
    page = cache_kv.shape[2]
    dec_seq = rows // N_HEADS
    grid_spec = pltpu.PrefetchScalarGridSpec(
        num_scalar_prefetch=1,
        grid=(db // nb,),
        in_specs=[pl.BlockSpec((nb, rows, KCAT), lambda i, pt: (i, 0, 0)),
                  pl.BlockSpec((nb, LANES, KCAT), lambda i, pt: (i, 0, 0)),
                  pl.BlockSpec(memory_space=pl.ANY),
                  pl.BlockSpec(memory_space=pl.ANY)],
        out_specs=pl.BlockSpec((nb, rows, KV_LORA), lambda i, pt: (i, 0, 0)),
        scratch_shapes=[pltpu.VMEM((2, nb, n_pages * page, KV_LORA), F32),
                        pltpu.VMEM((2, nb, ROPE_DIM, n_pages * page), F32),
                        pltpu.SemaphoreType.DMA((2, 2))],
    )
    return pl.pallas_call(
        functools.partial(_attn_sample_kernel, layer=layer, n_pages=n_pages, page=page,
                          dec_seq=dec_seq, nb=nb),
        grid_spec=grid_spec,
        out_shape=jax.ShapeDtypeStruct((db, rows, KV_LORA), BF16),
        compiler_params=pltpu.CompilerParams(dimension_semantics=("arbitrary",),
                                             vmem_limit_bytes=VMEM_LIMIT),
        name="attn_sample",
    )(page_table, qs, knew, cache_kv, cache_krt)


def _post_kernel(x_ref, ol_ref, g_ref, wo_ref, wg_ref, nmp_ref, nfp_ref, nfo_ref, wup_ref, wdn_ref,
                 y_ref, *, ff_chunk):
    mix = _dot(ol_ref[...], wo_ref[...]) + _dot(g_ref[...], wg_ref[...])
    x1 = x_ref[...] + _rms(mix, nmp_ref[...])
    h2 = _rms(x1, nfp_ref[...]).astype(BF16)
    d_ff = wup_ref.shape[1]
    f = None
    for c in range(d_ff // ff_chunk):
        cs = slice(c * ff_chunk, (c + 1) * ff_chunk)
        hf = _dot(h2, wup_ref[:, cs])
        act = jnp.square(jnp.maximum(hf, 0.0)).astype(BF16)
        part = _dot(act, wdn_ref[cs, :])
        f = part if f is None else f + part
    y_ref[...] = x1 + _rms(f, nfo_ref[...])


def _post(x, ol, gout, w, *, layer, tm, ff_chunk):
    n, d = x.shape

    def const(a):
        if a.ndim == 3:
            return pl.BlockSpec((None,) + a.shape[1:], lambda i: (layer, 0, 0),
                                pipeline_mode=pl.Buffered(1))
        return pl.BlockSpec(a.shape, lambda i: (0, 0), pipeline_mode=pl.Buffered(1))

    rows = lambda wdt: pl.BlockSpec((tm, wdt), lambda i: (i, 0))
    ws = [w["w_o"], w["w_out_g"], w["nm_post"], w["nf_pre"], w["nf_post"], w["w_up"], w["w_down"]]
    return pl.pallas_call(
        functools.partial(_post_kernel, ff_chunk=ff_chunk),
        grid=(n // tm,),
        in_specs=[rows(d), rows(QABS_W), rows(GM_WIDTH)] + [const(a) for a in ws],
        out_specs=rows(d),
        out_shape=jax.ShapeDtypeStruct((n, d), F32),
        compiler_params=pltpu.CompilerParams(dimension_semantics=("parallel",),
                                             vmem_limit_bytes=VMEM_LIMIT),
        name="post",
    )(x, ol, gout, *ws)


def _rope_tables(pos):
    half = ROPE_DIM // 2
    freqs = ROPE_THETA ** (-np.arange(half, dtype=np.float64) / half)
    ang = pos.astype(np.float64)[:, None] * freqs[None, :]
    tile = lambda t: jnp.asarray(np.tile(t.astype(np.float32), (1, 2 * N_HEADS)))
    return tile(np.cos(ang)), tile(np.sin(ang))


def _swap_halves(w):
    half = w.shape[-1] // 2
    return jnp.concatenate([-w[..., half:], w[..., :half]], axis=-1)


def _pair_groups(g):
    n = g.shape[0]
    return g.reshape(n // 2, 2, CHUNK, CHUNK).transpose(0, 2, 1, 3).reshape(n // 2, CHUNK, 2 * CHUNK)


def kernel(x_prompt, x_sample, cache_kv_latent, cache_k_rope, page_table, norm_mix_pre, norm_mix_post,
           w_in, q_norm, w_uq, kv_norm, w_uk, w_uv, v_norm, w_spatial, b_spatial, w_out, norm_ffn_pre,
           norm_ffn_post, w_ff_up, w_ff_down):
    batch, seq, d_model = x_prompt.shape
    dec_batch, dec_seq, _ = x_sample.shape
    depth = w_in.shape[0]
    n_pages, page = page_table.shape[1], cache_kv_latent.shape[2]
    past_len = n_pages * page
    tm = 512
    attn_blk = 256
    attn_kblk = 512
    attn_groups = 8
    attn_rchunk = 32
    ff_chunk = 1024
    sample_nb = 2

    cos_p, sin_p = _rope_tables(np.arange(seq))
    cos_s, sin_s = _rope_tables(past_len + np.arange(dec_batch * dec_seq) % dec_seq)
    cache_krt = jnp.swapaxes(cache_k_rope, 2, 3)

    w_abs, w_o = _prep(w_uq, w_uk, w_uv, w_out)
    w_out_g = w_out[:, N_HEADS * V_DIM:].astype(BF16)
    w_up = w_ff_up.astype(BF16)
    w_down = w_ff_down.astype(BF16)

    xp = x_prompt.reshape(batch * seq, d_model)
    xs = x_sample.reshape(dec_batch * dec_seq, d_model)
    ckv_p, kpe_p, ckv_s, kpe_s, v_s = [], [], [], [], []
    row = lambda a: a.reshape(1, -1)
    for l in range(depth):
        wi = w_in[l]
        kpe_w = wi[:, Q_LORA + KV_LORA:Q_LORA + KV_LORA + ROPE_DIM]
        pad = jnp.zeros((d_model, LANES - ROPE_DIM), F32)
        w_in_r = jnp.concatenate(
            [wi[:, :Q_LORA + KV_LORA], wi[:, Q_LORA + KV_LORA + ROPE_DIM:], kpe_w, pad,
             _swap_halves(kpe_w), pad], axis=1).astype(BF16)
        wq_pe = w_uq[l][:, :, NOPE_DIM:]
        w_q = jnp.concatenate(
            [w_abs[l], wq_pe.reshape(Q_LORA, QPE_W).astype(BF16),
             _swap_halves(wq_pe).reshape(Q_LORA, QPE_W).astype(BF16)], axis=1)
        w = dict(nm_pre=row(norm_mix_pre[l]), w_in=w_in_r, q_norm=row(q_norm[l]),
                 kv_norm=row(kv_norm[l]), v_norm=row(v_norm[l]), w_q=w_q,
                 w_o=w_o, w_out_g=w_out_g,
                 nm_post=row(norm_mix_post[l]), nf_pre=row(norm_ffn_pre[l]),
                 nf_post=row(norm_ffn_post[l]), w_up=w_up, w_down=w_down)
        ws, bs = w_spatial[l], b_spatial[l]

        gp = _pair_groups(ws)
        gb = jnp.repeat(bs.T, GM_HEAD_DIM, axis=1)
        qabs, qpe, kcat, ckv, kpe, gout = _proj(xp, w, cos_p, sin_p, gp, gb, period=CHUNK,
                                                emit_v=False, tm=tm)
        ol = _attn_prompt(qabs, qpe, kcat, batch=batch, seq=seq, blk=attn_blk, kblk=attn_kblk,
                          groups=attn_groups, rchunk=attn_rchunk)
        xp = _post(xp, ol, gout, w, layer=l, tm=tm, ff_chunk=ff_chunk)
        ckv_p.append(ckv.reshape(batch, seq // page, page, KV_LORA))
        kpe_p.append(kpe.reshape(batch, seq // page, page, ROPE_DIM))

        reps = CHUNK // dec_seq
        gp = _pair_groups(jnp.tile(jnp.tile(ws[:, :dec_seq, :dec_seq], (1, 1, reps)), (1, reps, 1)))
        gb = jnp.tile(jnp.repeat(bs[:, :dec_seq].T, GM_HEAD_DIM, axis=1), (reps, 1))
        qabs, qpe, kcat, ckv, kpe, gout, vn = _proj(xs, w, cos_s, sin_s, gp, gb, period=dec_seq,
                                                    emit_v=True, tm=dec_batch * dec_seq)
        heads_to_rows = lambda a, wd: a.reshape(dec_batch, dec_seq, N_HEADS, wd).transpose(
            0, 2, 1, 3).reshape(dec_batch, N_HEADS * dec_seq, wd)
        qs = jnp.concatenate([heads_to_rows(qabs, KV_LORA), heads_to_rows(qpe.astype(BF16), ROPE_DIM)],
                             axis=-1)
        knew = jnp.pad(kcat.reshape(dec_batch, dec_seq, KCAT), ((0, 0), (0, LANES - dec_seq), (0, 0)))
        o = _attn_sample(page_table, qs, knew, cache_kv_latent, cache_krt, layer=l, nb=sample_nb)
        ol = o.reshape(dec_batch, N_HEADS, dec_seq, KV_LORA).transpose(0, 2, 1, 3).reshape(
            dec_batch * dec_seq, QABS_W)
        xs = _post(xs, ol, gout, w, layer=l, tm=dec_batch * dec_seq, ff_chunk=ff_chunk)
        ckv_s.append(ckv.reshape(dec_batch, dec_seq, KV_LORA))
        kpe_s.append(kpe.reshape(dec_batch, dec_seq, ROPE_DIM))
        v_s.append(vn.reshape(dec_batch, dec_seq, GM_WIDTH))

    return (xp.reshape(batch, seq, d_model), xs.reshape(dec_batch, dec_seq, d_model),
            jnp.stack(ckv_p), jnp.stack(kpe_p), jnp.stack(ckv_s), jnp.stack(kpe_s), jnp.stack(v_s))
```

```python
import functools

import jax
import jax.numpy as jnp
import numpy as np
from jax import lax
from jax.experimental import pallas as pl
from jax.experimental.pallas import tpu as pltpu

N_HEADS = 8
NOPE_DIM = 64
ROPE_DIM = 32
V_DIM = 64
Q_LORA = 256
KV_LORA = 128
GM_GROUPS = 8
GM_HEAD_DIM = 64
GM_WIDTH = GM_GROUPS * GM_HEAD_DIM
CHUNK = 128
ROPE_THETA = 10000.0
EPS = 1e-6
ATTN_SCALE = (NOPE_DIM + ROPE_DIM) ** -0.5
LOG2E = 1.4426950408889634
Q_SCALE = ATTN_SCALE * LOG2E
NEG_INF = -1e30

LANES = 128
KCAT = KV_LORA + ROPE_DIM
QABS_W = N_HEADS * KV_LORA
QPE_W = N_HEADS * ROPE_DIM

C_QLAT = 0
C_CKV = Q_LORA
C_U = C_CKV + KV_LORA
C_V = C_U + GM_WIDTH
C_KPE = C_V + GM_WIDTH
C_KPE_SW = C_KPE + ROPE_DIM
IN_W = C_KPE + LANES

VMEM_LIMIT = 56 * 1024 * 1024

F32 = jnp.float32
BF16 = jnp.bfloat16


def _rms(x, g):
    ms = jnp.mean(x * x, axis=-1, keepdims=True)
    return x * lax.rsqrt(ms + EPS) * g


def _dot(a, b):
    return jnp.dot(a, b, preferred_element_type=F32)


def _lane_tile(x, n):
    return jnp.concatenate([x] * n, axis=1)


def _dot_nt(a, b):
    return lax.dot_general(a, b, (((1,), (1,)), ((), ())), preferred_element_type=F32)


def _prep_kernel(wqn_ref, wuk_ref, wuv_ref, wouta_ref, wabs_ref, wo_ref):
    for h in range(N_HEADS):
        wabs = _dot(wqn_ref[0, h], wuk_ref[0, h]) * Q_SCALE
        wabs_ref[0, :, h * KV_LORA:(h + 1) * KV_LORA] = wabs.astype(BF16)
        wo = _dot(wuv_ref[0, h], wouta_ref[0, h])
        wo_ref[0, h * KV_LORA:(h + 1) * KV_LORA, :] = wo.astype(BF16)


def _prep(w_uq, w_uk, w_uv, w_out):
    depth, _, _, _ = w_uq.shape
    d_model = w_out.shape[-1]
    wqn = jnp.transpose(w_uq[..., :NOPE_DIM], (0, 2, 1, 3))
    wuk = jnp.transpose(w_uk, (0, 2, 3, 1))
    wuv = jnp.transpose(w_uv, (0, 2, 1, 3))
    wouta = w_out[:, :N_HEADS * V_DIM].reshape(depth, N_HEADS, V_DIM, d_model)
    blk = lambda shp: pl.BlockSpec((1,) + shp, lambda l: (l,) + (0,) * len(shp))
    return pl.pallas_call(
        _prep_kernel,
        grid=(depth,),
        in_specs=[blk((N_HEADS, Q_LORA, NOPE_DIM)), blk((N_HEADS, NOPE_DIM, KV_LORA)),
                  blk((N_HEADS, KV_LORA, V_DIM)), blk((N_HEADS, V_DIM, d_model))],
        out_specs=[blk((Q_LORA, QABS_W)), blk((QABS_W, d_model))],
        out_shape=[jax.ShapeDtypeStruct((depth, Q_LORA, QABS_W), BF16),
                   jax.ShapeDtypeStruct((depth, QABS_W, d_model), BF16)],
        name="prep_fold",
    )(wqn, wuk, wuv, wouta)


def _proj_kernel(x_ref, nm_ref, win_ref, qn_ref, kvn_ref, vnw_ref, wq_ref, cos_ref, sin_ref,
                 gp_ref, gb_ref,
                 qst_ref, kcat_ref, ckv_ref, kpe_ref, gout_ref, *vn_out, period, qblk):
    tm = x_ref.shape[0]
    h = _rms(x_ref[...], nm_ref[...]).astype(BF16)
    z = _dot(h, win_ref[...])

    ql = _rms(z[:, C_QLAT:C_QLAT + Q_LORA], qn_ref[...]).astype(BF16)
    q = _dot(ql, wq_ref[...])
    cos = cos_ref[...]
    sin = sin_ref[...]
    qpe = (q[:, QABS_W:QABS_W + QPE_W] * cos + q[:, QABS_W + QPE_W:] * sin) * Q_SCALE
    for b in range(tm // qblk):
        rs = slice(b * qblk, (b + 1) * qblk)
        for hd in range(N_HEADS):
            hs = slice(hd * qblk, (hd + 1) * qblk)
            qst_ref[b, hs, :KV_LORA] = q[rs, hd * KV_LORA:(hd + 1) * KV_LORA].astype(BF16)
            qst_ref[b, hs, KV_LORA:] = qpe[rs, hd * ROPE_DIM:(hd + 1) * ROPE_DIM].astype(BF16)

    ckv = _rms(z[:, C_CKV:C_CKV + KV_LORA], kvn_ref[...])
    ckv_ref[...] = ckv
    kpe = (z[:, C_KPE:C_KPE + ROPE_DIM] * cos[:, :ROPE_DIM]
           + z[:, C_KPE_SW:C_KPE_SW + ROPE_DIM] * sin[:, :ROPE_DIM])
    kpe_ref[...] = kpe
    kcat_ref[:, :KV_LORA] = ckv.astype(BF16)
    kcat_ref[:, KV_LORA:] = kpe.astype(BF16)

    vn = _rms(z[:, C_V:C_V + GM_WIDTH], vnw_ref[...])
    if vn_out:
        vn_out[0][...] = vn
    row = lax.broadcasted_iota(jnp.int32, (CHUNK, 2 * CHUNK), 0)
    col = lax.broadcasted_iota(jnp.int32, (CHUNK, 2 * CHUNK), 1) & (CHUNK - 1)
    mask = col <= row
    if period < CHUNK:
        sh = period.bit_length() - 1
        mask = mask & ((row >> sh) == (col >> sh))
    left = lax.broadcasted_iota(jnp.int32, (CHUNK, LANES), 1) < GM_HEAD_DIM
    for k in range(GM_GROUPS // 2):
        gpk = jnp.where(mask, gp_ref[k], 0.0).astype(BF16)
        cs = slice(k * LANES, (k + 1) * LANES)
        for c in range(tm // CHUNK):
            rs = slice(c * CHUNK, (c + 1) * CHUNK)
            vcol = vn[rs, cs]
            rhs = jnp.concatenate([jnp.where(left, vcol, 0.0), jnp.where(left, 0.0, vcol)],
                                  axis=0).astype(BF16)
            mixed = _dot(gpk, rhs) + gb_ref[:, cs]
            gout_ref[rs, cs] = (z[rs, C_U + k * LANES:C_U + (k + 1) * LANES] * mixed).astype(BF16)


def _proj(x, w, cos, sin, gp, gb, *, period, emit_v, tm, qblk):
    n, d = x.shape
    n_pos_tiles = cos.shape[0] // tm
    const = lambda a: pl.BlockSpec(a.shape, lambda i: (0,) * a.ndim)
    rows = lambda wdt: pl.BlockSpec((tm, wdt), lambda i: (i, 0))
    pos = pl.BlockSpec((tm, QPE_W), lambda i: (i % n_pos_tiles, 0))
    out_w = [(KCAT, BF16), (KV_LORA, F32), (ROPE_DIM, F32), (GM_WIDTH, BF16)]
    if emit_v:
        out_w.append((GM_WIDTH, F32))
    ins = [x, w["nm_pre"], w["w_in"], w["q_norm"], w["kv_norm"], w["v_norm"], w["w_q"]]
    qst_rows = N_HEADS * qblk
    return pl.pallas_call(
        functools.partial(_proj_kernel, period=period, qblk=qblk),
        grid=(n // tm,),
        in_specs=[rows(d)] + [const(a) for a in ins[1:]] + [pos, pos, const(gp), const(gb)],
        out_specs=[pl.BlockSpec((tm // qblk, qst_rows, KCAT), lambda i: (i, 0, 0))]
        + [rows(wd) for wd, _ in out_w],
        out_shape=[jax.ShapeDtypeStruct((n // qblk, qst_rows, KCAT), BF16)]
        + [jax.ShapeDtypeStruct((n, wd), dt) for wd, dt in out_w],
        compiler_params=pltpu.CompilerParams(dimension_semantics=("parallel",),
                                             vmem_limit_bytes=VMEM_LIMIT),
        name="proj",
    )(*ins, cos, sin, gp, gb)


def _attn_prompt_kernel(q_ref, k_ref, o_ref, s_scr, p_scr, m_scr, a_scr, acc_scr,
                        *, blk, kblk, groups, rchunk):
    gheads = N_HEADS // groups
    grows = gheads * blk
    qb = pl.program_id(1)
    m_scr[...] = jnp.full(m_scr.shape, NEG_INF, F32)
    acc_scr[...] = jnp.zeros(acc_scr.shape, F32)
    n_full = (qb * blk) // kblk

    def max_rows(r0, k0, width, masked):
        rows = slice(r0, r0 + rchunk)
        s = s_scr[rows, :width]
        if masked:
            qpos = qb * blk + ((r0 + lax.broadcasted_iota(jnp.int32, s.shape, 0)) & (blk - 1))
            kpos = k0 + lax.broadcasted_iota(jnp.int32, s.shape, 1)
            s = jnp.where(kpos <= qpos, s, NEG_INF)
            s_scr[rows, :width] = s
        m_old = m_scr[rows, :]
        m_new = jnp.maximum(m_old, jnp.max(s, axis=-1, keepdims=True))
        a_scr[rows, :] = jnp.exp2(m_old - m_new)
        m_scr[rows, :] = m_new

    def prob_rows(r0, width):
        rows = slice(r0, r0 + rchunk)
        x = s_scr[rows, :width] - _lane_tile(m_scr[rows, :], width // LANES)
        p_scr[rows, :width] = jnp.exp2(x.astype(BF16))

    def key_block(k0, width, masked):
        kt = k_ref[pl.ds(pl.multiple_of(k0, width), width), :]
        vext = jnp.concatenate([kt[:, :KV_LORA], jnp.ones((width, LANES), BF16)], axis=1)
        for g in range(groups):
            grp = slice(g * grows, (g + 1) * grows)
            s_scr[grp, :width] = _dot_nt(q_ref[grp, :], kt)
        for g in range(groups):
            grp = slice(g * grows, (g + 1) * grows)
            for c in range(grows // rchunk):
                max_rows(g * grows + c * rchunk, k0, width, masked)
            for c in range(grows // rchunk):
                prob_rows(g * grows + c * rchunk, width)
            acc_scr[grp, :] = (_lane_tile(a_scr[grp, :], 2) * acc_scr[grp, :]
                               + _dot(p_scr[grp, :width], vext))

    def body(kb, carry):
        key_block(kb * kblk, kblk, False)
        return carry

    lax.fori_loop(0, n_full, body, 0)
    k_tail = n_full * kblk
    aligned = k_tail == qb * blk

    @pl.when(aligned)
    def _():
        key_block(k_tail, blk, True)

    @pl.when(jnp.logical_not(aligned))
    def _():
        key_block(k_tail, kblk, True)

    o = (acc_scr[:, :KV_LORA] / acc_scr[:, KV_LORA:]).astype(BF16)
    for h in range(N_HEADS):
        o_ref[:, h * KV_LORA:(h + 1) * KV_LORA] = o[h * blk:(h + 1) * blk, :]


def _attn_prompt(qst, kcat, *, batch, seq, blk, kblk, groups, rchunk):
    nq = seq // blk
    rows_all = N_HEADS * blk
    return pl.pallas_call(
        functools.partial(_attn_prompt_kernel, blk=blk, kblk=kblk, groups=groups, rchunk=rchunk),
        grid=(batch, nq),
        in_specs=[pl.BlockSpec((None, rows_all, KCAT), lambda b, i: (b * nq + i, 0, 0)),
                  pl.BlockSpec((seq, KCAT), lambda b, i: (b, 0))],
        out_specs=pl.BlockSpec((blk, QABS_W), lambda b, i: (b * nq + i, 0)),
        out_shape=jax.ShapeDtypeStruct((batch * seq, QABS_W), BF16),
        scratch_shapes=[pltpu.VMEM((rows_all, kblk), F32),
                        pltpu.VMEM((rows_all, kblk), BF16),
                        pltpu.VMEM((rows_all, LANES), F32),
                        pltpu.VMEM((rows_all, LANES), F32),
                        pltpu.VMEM((rows_all, KV_LORA + LANES), F32)],
        compiler_params=pltpu.CompilerParams(dimension_semantics=("parallel", "parallel"),
                                             vmem_limit_bytes=VMEM_LIMIT),
        name="attn_prompt",
    )(qst, kcat)


def _attn_sample_kernel(pt_ref, q_ref, knew_ref, ckv_hbm, krt_hbm, o_ref, kvbuf, krbuf, sems,
                        *, layer, n_pages, page, dec_seq, nb):
    i = pl.program_id(0)
    nsteps = pl.num_programs(0)
    slot = i % 2

    def page_copies(pg, j, p, sl):
        cols = pl.ds(p * page, page)
        return (pltpu.make_async_copy(ckv_hbm.at[layer, pg], kvbuf.at[sl, j, cols], sems.at[0, sl]),
                pltpu.make_async_copy(krt_hbm.at[layer, pg], krbuf.at[sl, j, :, cols], sems.at[1, sl]))

    def start_all(step, sl):
        for j in range(nb):
            for p in range(n_pages):
                for cp in page_copies(pt_ref[step * nb + j, p], j, p, sl):
                    cp.start()

    @pl.when(i == 0)
    def _():
        start_all(0, 0)

    @pl.when(i + 1 < nsteps)
    def _():
        start_all(i + 1, 1 - slot)

    for j in range(nb):
        for p in range(n_pages):
            for cp in page_copies(0, j, p, slot):
                cp.wait()

    rows = q_ref.shape[1]
    tpos = lax.broadcasted_iota(jnp.int32, (rows, LANES), 0) & (dec_seq - 1)
    jpos = lax.broadcasted_iota(jnp.int32, (rows, LANES), 1)
    new_visible = jpos <= tpos
    for j in range(nb):
        q = q_ref[j]
        kv = kvbuf[slot, j].astype(BF16)
        krt = krbuf[slot, j].astype(BF16)
        s_past = _dot_nt(q[:, :KV_LORA], kv) + _dot(q[:, KV_LORA:], krt)
        knew = knew_ref[j]
        s_new = jnp.where(new_visible, _dot_nt(q, knew), NEG_INF)
        m = jnp.maximum(jnp.max(s_past, axis=-1, keepdims=True),
                        jnp.max(s_new, axis=-1, keepdims=True))
        p_past = jnp.exp2(s_past - m)
        p_new = jnp.exp2(s_new - m)
        l = jnp.sum(p_past, axis=-1, keepdims=True) + jnp.sum(p_new, axis=-1, keepdims=True)
        acc = _dot(p_past.astype(BF16), kv) + _dot(p_new.astype(BF16), knew[:, :KV_LORA])
        o_ref[j] = (acc / l).astype(BF16)


def _attn_sample(page_table, qs, knew, cache_kv, cache_krt, *, layer, nb):
    db, rows, _ = qs.shape
    n_pages = page_table.shape[1]
    page = cache_kv.shape[2]
    dec_seq = rows // N_HEADS
    grid_spec = pltpu.PrefetchScalarGridSpec(
        num_scalar_prefetch=1,
        grid=(db // nb,),
        in_specs=[pl.BlockSpec((nb, rows, KCAT), lambda i, pt: (i, 0, 0)),
                  pl.BlockSpec((nb, LANES, KCAT), lambda i, pt: (i, 0, 0)),
                  pl.BlockSpec(memory_space=pl.ANY),
                  pl.BlockSpec(memory_space=pl.ANY)],
        out_specs=pl.BlockSpec((nb, rows, KV_LORA), lambda i, pt: (i, 0, 0)),
        scratch_shapes=[pltpu.VMEM((2, nb, n_pages * page, KV_LORA), F32),
                        pltpu.VMEM((2, nb, ROPE_DIM, n_pages * page), F32),
                        pltpu.SemaphoreType.DMA((2, 2))],
    )
    return pl.pallas_call(
        functools.partial(_attn_sample_kernel, layer=layer, n_pages=n_pages, page=page,
                          dec_seq=dec_seq, nb=nb),
        grid_spec=grid_spec,
        out_shape=jax.ShapeDtypeStruct((db, rows, KV_LORA), BF16),
        compiler_params=pltpu.CompilerParams(dimension_semantics=("arbitrary",),
                                             vmem_limit_bytes=VMEM_LIMIT),
        name="attn_sample",
    )(page_table, qs, knew, cache_kv, cache_krt)


def _post_kernel(x_ref, ol_ref, g_ref, wo_ref, wg_ref, nmp_ref, nfp_ref, nfo_ref, wup_ref, wdn_ref,
                 y_ref, *, ff_chunk):
    mix = _dot(ol_ref[...], wo_ref[...]) + _dot(g_ref[...], wg_ref[...])
    x1 = x_ref[...] + _rms(mix, nmp_ref[...])
    h2 = _rms(x1, nfp_ref[...]).astype(BF16)
    d_ff = wup_ref.shape[1]
    f = None
    for c in range(d_ff // ff_chunk):
        cs = slice(c * ff_chunk, (c + 1) * ff_chunk)
        hf = _dot(h2, wup_ref[:, cs])
        act = jnp.square(jnp.maximum(hf, 0.0)).astype(BF16)
        part = _dot(act, wdn_ref[cs, :])
        f = part if f is None else f + part
    y_ref[...] = x1 + _rms(f, nfo_ref[...])


def _post(x, ol, gout, w, *, layer, tm, ff_chunk):
    n, d = x.shape

    def const(a):
        if a.ndim == 3:
            return pl.BlockSpec((None,) + a.shape[1:], lambda i: (layer, 0, 0),
                                pipeline_mode=pl.Buffered(1))
        return pl.BlockSpec(a.shape, lambda i: (0, 0), pipeline_mode=pl.Buffered(1))

    rows = lambda wdt: pl.BlockSpec((tm, wdt), lambda i: (i, 0))
    ws = [w["w_o"], w["w_out_g"], w["nm_post"], w["nf_pre"], w["nf_post"], w["w_up"], w["w_down"]]
    return pl.pallas_call(
        functools.partial(_post_kernel, ff_chunk=ff_chunk),
        grid=(n // tm,),
        in_specs=[rows(d), rows(QABS_W), rows(GM_WIDTH)] + [const(a) for a in ws],
        out_specs=rows(d),
        out_shape=jax.ShapeDtypeStruct((n, d), F32),
        compiler_params=pltpu.CompilerParams(dimension_semantics=("parallel",),
                                             vmem_limit_bytes=VMEM_LIMIT),
        name="post",
    )(x, ol, gout, *ws)


def _rope_tables(pos):
    half = ROPE_DIM // 2
    freqs = ROPE_THETA ** (-np.arange(half, dtype=np.float64) / half)
    ang = pos.astype(np.float64)[:, None] * freqs[None, :]
    tile = lambda t: jnp.asarray(np.tile(t.astype(np.float32), (1, 2 * N_HEADS)))
    return tile(np.cos(ang)), tile(np.sin(ang))


def _swap_halves(w):
    half = w.shape[-1] // 2
    return jnp.concatenate([-w[..., half:], w[..., :half]], axis=-1)


def _pair_groups(g):
    n = g.shape[0]
    return g.reshape(n // 2, 2, CHUNK, CHUNK).transpose(0, 2, 1, 3).reshape(n // 2, CHUNK, 2 * CHUNK)


def kernel(x_prompt, x_sample, cache_kv_latent, cache_k_rope, page_table, norm_mix_pre, norm_mix_post,
           w_in, q_norm, w_uq, kv_norm, w_uk, w_uv, v_norm, w_spatial, b_spatial, w_out, norm_ffn_pre,
           norm_ffn_post, w_ff_up, w_ff_down):
    batch, seq, d_model = x_prompt.shape
    dec_batch, dec_seq, _ = x_sample.shape
    depth = w_in.shape[0]
    n_pages, page = page_table.shape[1], cache_kv_latent.shape[2]
    past_len = n_pages * page
    tm = 512
    attn_blk = 256
    attn_kblk = 512
    attn_groups = 8
    attn_rchunk = 32
    ff_chunk = 1024
    sample_nb = 2

    cos_p, sin_p = _rope_tables(np.arange(seq))
    cos_s, sin_s = _rope_tables(past_len + np.arange(dec_batch * dec_seq) % dec_seq)
    cache_krt = jnp.swapaxes(cache_k_rope, 2, 3)

    w_abs, w_o = _prep(w_uq, w_uk, w_uv, w_out)
    w_out_g = w_out[:, N_HEADS * V_DIM:].astype(BF16)
    w_up = w_ff_up.astype(BF16)
    w_down = w_ff_down.astype(BF16)

    xp = x_prompt.reshape(batch * seq, d_model)
    xs = x_sample.reshape(dec_batch * dec_seq, d_model)
    ckv_p, kpe_p, ckv_s, kpe_s, v_s = [], [], [], [], []
    row = lambda a: a.reshape(1, -1)
    for l in range(depth):
        wi = w_in[l]
        kpe_w = wi[:, Q_LORA + KV_LORA:Q_LORA + KV_LORA + ROPE_DIM]
        pad = jnp.zeros((d_model, LANES - 2 * ROPE_DIM), F32)
        w_in_r = jnp.concatenate(
            [wi[:, :Q_LORA + KV_LORA], wi[:, Q_LORA + KV_LORA + ROPE_DIM:], kpe_w,
             _swap_halves(kpe_w), pad], axis=1).astype(BF16)
        wq_pe = w_uq[l][:, :, NOPE_DIM:]
        w_q = jnp.concatenate(
            [w_abs[l], wq_pe.reshape(Q_LORA, QPE_W).astype(BF16),
             _swap_halves(wq_pe).reshape(Q_LORA, QPE_W).astype(BF16)], axis=1)
        w = dict(nm_pre=row(norm_mix_pre[l]), w_in=w_in_r, q_norm=row(q_norm[l]),
                 kv_norm=row(kv_norm[l]), v_norm=row(v_norm[l]), w_q=w_q,
                 w_o=w_o, w_out_g=w_out_g,
                 nm_post=row(norm_mix_post[l]), nf_pre=row(norm_ffn_pre[l]),
                 nf_post=row(norm_ffn_post[l]), w_up=w_up, w_down=w_down)
        ws, bs = w_spatial[l], b_spatial[l]

        gp = _pair_groups(ws)
        gb = jnp.repeat(bs.T, GM_HEAD_DIM, axis=1)
        qst, kcat, ckv, kpe, gout = _proj(xp, w, cos_p, sin_p, gp, gb, period=CHUNK, emit_v=False,
                                          tm=tm, qblk=attn_blk)
        ol = _attn_prompt(qst, kcat, batch=batch, seq=seq, blk=attn_blk, kblk=attn_kblk,
                          groups=attn_groups, rchunk=attn_rchunk)
        xp = _post(xp, ol, gout, w, layer=l, tm=tm, ff_chunk=ff_chunk)
        ckv_p.append(ckv.reshape(batch, seq // page, page, KV_LORA))
        kpe_p.append(kpe.reshape(batch, seq // page, page, ROPE_DIM))

        reps = CHUNK // dec_seq
        gp = _pair_groups(jnp.tile(jnp.tile(ws[:, :dec_seq, :dec_seq], (1, 1, reps)), (1, reps, 1)))
        gb = jnp.tile(jnp.repeat(bs[:, :dec_seq].T, GM_HEAD_DIM, axis=1), (reps, 1))
        qst, kcat, ckv, kpe, gout, vn = _proj(xs, w, cos_s, sin_s, gp, gb, period=dec_seq,
                                              emit_v=True, tm=dec_batch * dec_seq, qblk=attn_blk)
        seqs_blk = attn_blk // dec_seq
        qs = qst.reshape(-1, N_HEADS, seqs_blk, dec_seq, KCAT).transpose(0, 2, 1, 3, 4).reshape(
            dec_batch, N_HEADS * dec_seq, KCAT)
        knew = jnp.pad(kcat.reshape(dec_batch, dec_seq, KCAT), ((0, 0), (0, LANES - dec_seq), (0, 0)))
        o = _attn_sample(page_table, qs, knew, cache_kv_latent, cache_krt, layer=l, nb=sample_nb)
        ol = o.reshape(dec_batch, N_HEADS, dec_seq, KV_LORA).transpose(0, 2, 1, 3).reshape(
            dec_batch * dec_seq, QABS_W)
        xs = _post(xs, ol, gout, w, layer=l, tm=dec_batch * dec_seq, ff_chunk=ff_chunk)
        ckv_s.append(ckv.reshape(dec_batch, dec_seq, KV_LORA))
        kpe_s.append(kpe.reshape(dec_batch, dec_seq, ROPE_DIM))
        v_s.append(vn.reshape(dec_batch, dec_seq, GM_WIDTH))

    return (xp.reshape(batch, seq, d_model), xs.reshape(dec_batch, dec_seq, d_model),
            jnp.stack(ckv_p), jnp.stack(kpe_p), jnp.stack(ckv_s), jnp.stack(kpe_s), jnp.stack(v_s))
```

```python
import functools

import jax
import jax.numpy as jnp
import numpy as np
from jax import lax
from jax.experimental import pallas as pl
from jax.experimental.pallas import tpu as pltpu

N_HEADS = 8
NOPE_DIM = 64
ROPE_DIM = 32
V_DIM = 64
Q_LORA = 256
KV_LORA = 128
GM_GROUPS = 8
GM_HEAD_DIM = 64
GM_WIDTH = GM_GROUPS * GM_HEAD_DIM
CHUNK = 128
ROPE_THETA = 10000.0
EPS = 1e-6
ATTN_SCALE = (NOPE_DIM + ROPE_DIM) ** -0.5
LOG2E = 1.4426950408889634
Q_SCALE = ATTN_SCALE * LOG2E
NEG_INF = -1e30

LANES = 128
KCAT = KV_LORA + ROPE_DIM
QABS_W = N_HEADS * KV_LORA
QPE_W = N_HEADS * ROPE_DIM

C_QLAT = 0
C_CKV = Q_LORA
C_U = C_CKV + KV_LORA
C_V = C_U + GM_WIDTH
C_KPE = C_V + GM_WIDTH
C_KPE_SW = C_KPE + ROPE_DIM
IN_W = C_KPE + LANES

VMEM_LIMIT = 56 * 1024 * 1024

F32 = jnp.float32
BF16 = jnp.bfloat16


def _rms(x, g):
    ms = jnp.mean(x * x, axis=-1, keepdims=True)
    return x * lax.rsqrt(ms + EPS) * g


def _dot(a, b):
    return jnp.dot(a, b, preferred_element_type=F32)


def _lane_tile(x, n):
    return jnp.concatenate([x] * n, axis=1)


def _dot_nt(a, b):
    return lax.dot_general(a, b, (((1,), (1,)), ((), ())), preferred_element_type=F32)


def _prep_kernel(wqn_ref, wuk_ref, wuv_ref, wouta_ref, wabs_ref, wo_ref):
    for h in range(N_HEADS):
        wabs = _dot(wqn_ref[0, h], wuk_ref[0, h]) * Q_SCALE
        wabs_ref[0, :, h * KV_LORA:(h + 1) * KV_LORA] = wabs.astype(BF16)
        wo = _dot(wuv_ref[0, h], wouta_ref[0, h])
        wo_ref[0, h * KV_LORA:(h + 1) * KV_LORA, :] = wo.astype(BF16)


def _prep(w_uq, w_uk, w_uv, w_out):
    depth, _, _, _ = w_uq.shape
    d_model = w_out.shape[-1]
    wqn = jnp.transpose(w_uq[..., :NOPE_DIM], (0, 2, 1, 3))
    wuk = jnp.transpose(w_uk, (0, 2, 3, 1))
    wuv = jnp.transpose(w_uv, (0, 2, 1, 3))
    wouta = w_out[:, :N_HEADS * V_DIM].reshape(depth, N_HEADS, V_DIM, d_model)
    blk = lambda shp: pl.BlockSpec((1,) + shp, lambda l: (l,) + (0,) * len(shp))
    return pl.pallas_call(
        _prep_kernel,
        grid=(depth,),
        in_specs=[blk((N_HEADS, Q_LORA, NOPE_DIM)), blk((N_HEADS, NOPE_DIM, KV_LORA)),
                  blk((N_HEADS, KV_LORA, V_DIM)), blk((N_HEADS, V_DIM, d_model))],
        out_specs=[blk((Q_LORA, QABS_W)), blk((QABS_W, d_model))],
        out_shape=[jax.ShapeDtypeStruct((depth, Q_LORA, QABS_W), BF16),
                   jax.ShapeDtypeStruct((depth, QABS_W, d_model), BF16)],
        name="prep_fold",
    )(wqn, wuk, wuv, wouta)


def _proj_kernel(x_ref, nm_ref, win_ref, qn_ref, kvn_ref, vnw_ref, wq_ref, cos_ref, sin_ref,
                 gp_ref, gb_ref,
                 qst_ref, kcat_ref, ckv_ref, kpe_ref, gout_ref, *vn_out, period, qblk, sub):
    tm = x_ref.shape[0]
    row = lax.broadcasted_iota(jnp.int32, (CHUNK, 2 * CHUNK), 0)
    col = lax.broadcasted_iota(jnp.int32, (CHUNK, 2 * CHUNK), 1) & (CHUNK - 1)
    mask = col <= row
    if period < CHUNK:
        sh = period.bit_length() - 1
        mask = mask & ((row >> sh) == (col >> sh))
    left = lax.broadcasted_iota(jnp.int32, (CHUNK, LANES), 1) < GM_HEAD_DIM
    gpk = [jnp.where(mask, gp_ref[k], 0.0).astype(BF16) for k in range(GM_GROUPS // 2)]

    def rows_block(r0, nrows):
        ts = slice(r0, r0 + nrows)
        x = x_ref[ts, :]
        r_x = lax.rsqrt(jnp.mean(x * x, axis=-1, keepdims=True) + EPS)
        zr = _dot((x * nm_ref[...]).astype(BF16), win_ref[...])

        zq = zr[:, C_QLAT:C_QLAT + Q_LORA]
        r_q = lax.rsqrt(r_x * r_x * jnp.mean(zq * zq, axis=-1, keepdims=True) + EPS)
        q = _dot((zq * qn_ref[...]).astype(BF16), wq_ref[...]) * (r_x * r_q)
        z = zr * r_x
        cos = cos_ref[ts, :]
        sin = sin_ref[ts, :]
        qpe = (q[:, QABS_W:QABS_W + QPE_W] * cos + q[:, QABS_W + QPE_W:] * sin) * Q_SCALE
        for b in range(nrows // qblk):
            rs = slice(b * qblk, (b + 1) * qblk)
            for hd in range(N_HEADS):
                hs = slice(hd * qblk, (hd + 1) * qblk)
                qb = r0 // qblk + b
                qst_ref[qb, hs, :KV_LORA] = q[rs, hd * KV_LORA:(hd + 1) * KV_LORA].astype(BF16)
                qst_ref[qb, hs, KV_LORA:] = qpe[rs, hd * ROPE_DIM:(hd + 1) * ROPE_DIM].astype(BF16)

        ckv = _rms(z[:, C_CKV:C_CKV + KV_LORA], kvn_ref[...])
        ckv_ref[ts, :] = ckv
        kpe = (z[:, C_KPE:C_KPE + ROPE_DIM] * cos[:, :ROPE_DIM]
               + z[:, C_KPE_SW:C_KPE_SW + ROPE_DIM] * sin[:, :ROPE_DIM])
        kpe_ref[ts, :] = kpe
        kcat_ref[ts, :KV_LORA] = ckv.astype(BF16)
        kcat_ref[ts, KV_LORA:] = kpe.astype(BF16)

        vn = _rms(z[:, C_V:C_V + GM_WIDTH], vnw_ref[...])
        if vn_out:
            vn_out[0][ts, :] = vn
        for k in range(GM_GROUPS // 2):
            cs = slice(k * LANES, (k + 1) * LANES)
            for c in range(nrows // CHUNK):
                rs = slice(c * CHUNK, (c + 1) * CHUNK)
                vcol = vn[rs, cs]
                rhs = jnp.concatenate([jnp.where(left, vcol, 0.0), jnp.where(left, 0.0, vcol)],
                                      axis=0).astype(BF16)
                mixed = _dot(gpk[k], rhs) + gb_ref[:, cs]
                gout_ref[r0 + c * CHUNK:r0 + (c + 1) * CHUNK, cs] = (
                    z[rs, C_U + k * LANES:C_U + (k + 1) * LANES] * mixed).astype(BF16)

    for r0 in range(0, tm, sub):
        rows_block(r0, sub)


def _proj(x, w, cos, sin, gp, gb, *, period, emit_v, tm, qblk):
    n, d = x.shape
    n_pos_tiles = cos.shape[0] // tm
    const = lambda a: pl.BlockSpec(a.shape, lambda i: (0,) * a.ndim)
    rows = lambda wdt: pl.BlockSpec((tm, wdt), lambda i: (i, 0))
    pos = pl.BlockSpec((tm, QPE_W), lambda i: (i % n_pos_tiles, 0))
    out_w = [(KCAT, BF16), (KV_LORA, F32), (ROPE_DIM, F32), (GM_WIDTH, BF16)]
    if emit_v:
        out_w.append((GM_WIDTH, F32))
    ins = [x, w["nm_pre"], w["w_in"], w["q_norm"], w["kv_norm"], w["v_norm"], w["w_q"]]
    qst_rows = N_HEADS * qblk
    return pl.pallas_call(
        functools.partial(_proj_kernel, period=period, qblk=qblk, sub=qblk),
        grid=(n // tm,),
        in_specs=[rows(d)] + [const(a) for a in ins[1:]] + [pos, pos, const(gp), const(gb)],
        out_specs=[pl.BlockSpec((tm // qblk, qst_rows, KCAT), lambda i: (i, 0, 0))]
        + [rows(wd) for wd, _ in out_w],
        out_shape=[jax.ShapeDtypeStruct((n // qblk, qst_rows, KCAT), BF16)]
        + [jax.ShapeDtypeStruct((n, wd), dt) for wd, dt in out_w],
        compiler_params=pltpu.CompilerParams(dimension_semantics=("parallel",),
                                             vmem_limit_bytes=VMEM_LIMIT),
        name="proj",
    )(*ins, cos, sin, gp, gb)


def _attn_prompt_kernel(q_ref, k_ref, o_ref, s_scr, p_scr, m_scr, a_scr, acc_scr,
                        *, blk, kblk, groups, rchunk):
    gheads = N_HEADS // groups
    grows = gheads * blk
    qb = pl.program_id(1)
    m_scr[...] = jnp.full(m_scr.shape, NEG_INF, F32)
    acc_scr[...] = jnp.zeros(acc_scr.shape, F32)
    n_full = (qb * blk) // kblk

    def max_rows(r0, s, k0, masked):
        rows = slice(r0, r0 + rchunk)
        width = s.shape[1]
        if masked:
            qpos = qb * blk + ((r0 + lax.broadcasted_iota(jnp.int32, s.shape, 0)) & (blk - 1))
            kpos = k0 + lax.broadcasted_iota(jnp.int32, s.shape, 1)
            s = jnp.where(kpos <= qpos, s, NEG_INF)
        s_scr[rows, :width] = s
        m_old = m_scr[rows, :]
        m_new = jnp.maximum(m_old, jnp.max(s, axis=-1, keepdims=True))
        a_scr[rows, :] = jnp.exp2(m_old - m_new)
        m_scr[rows, :] = m_new

    def prob_rows(r0, width):
        rows = slice(r0, r0 + rchunk)
        x = s_scr[rows, :width] - _lane_tile(m_scr[rows, :], width // LANES)
        p_scr[rows, :width] = jnp.exp2(x.astype(BF16))

    def key_block(k0, width, masked):
        kt = k_ref[pl.ds(pl.multiple_of(k0, width), width), :]
        vext = jnp.concatenate([kt[:, :KV_LORA], jnp.ones((width, LANES), BF16)], axis=1)
        for g in range(groups):
            grp = slice(g * grows, (g + 1) * grows)
            s = _dot_nt(q_ref[grp, :], kt)
            for c in range(grows // rchunk):
                max_rows(g * grows + c * rchunk, s[c * rchunk:(c + 1) * rchunk, :], k0, masked)
        for g in range(groups):
            grp = slice(g * grows, (g + 1) * grows)
            for c in range(grows // rchunk):
                prob_rows(g * grows + c * rchunk, width)
            acc_scr[grp, :] = (_lane_tile(a_scr[grp, :], 2) * acc_scr[grp, :]
                               + _dot(p_scr[grp, :width], vext))

    def body(kb, carry):
        key_block(kb * kblk, kblk, False)
        return carry

    lax.fori_loop(0, n_full, body, 0)
    k_tail = n_full * kblk
    aligned = k_tail == qb * blk

    @pl.when(aligned)
    def _():
        key_block(k_tail, blk, True)

    @pl.when(jnp.logical_not(aligned))
    def _():
        key_block(k_tail, kblk, True)

    o = (acc_scr[:, :KV_LORA] / acc_scr[:, KV_LORA:]).astype(BF16)
    for h in range(N_HEADS):
        o_ref[:, h * KV_LORA:(h + 1) * KV_LORA] = o[h * blk:(h + 1) * blk, :]


def _attn_prompt(qst, kcat, *, batch, seq, blk, kblk, groups, rchunk):
    nq = seq // blk
    rows_all = N_HEADS * blk
    return pl.pallas_call(
        functools.partial(_attn_prompt_kernel, blk=blk, kblk=kblk, groups=groups, rchunk=rchunk),
        grid=(batch, nq),
        in_specs=[pl.BlockSpec((None, rows_all, KCAT), lambda b, i: (b * nq + i, 0, 0)),
                  pl.BlockSpec((seq, KCAT), lambda b, i: (b, 0))],
        out_specs=pl.BlockSpec((blk, QABS_W), lambda b, i: (b * nq + i, 0)),
        out_shape=jax.ShapeDtypeStruct((batch * seq, QABS_W), BF16),
        scratch_shapes=[pltpu.VMEM((rows_all, kblk), F32),
                        pltpu.VMEM((rows_all, kblk), BF16),
                        pltpu.VMEM((rows_all, LANES), F32),
                        pltpu.VMEM((rows_all, LANES), F32),
                        pltpu.VMEM((rows_all, KV_LORA + LANES), F32)],
        compiler_params=pltpu.CompilerParams(dimension_semantics=("parallel", "parallel"),
                                             vmem_limit_bytes=VMEM_LIMIT),
        name="attn_prompt",
    )(qst, kcat)


def _sample_fetch(pt_ref, ckv_hbm, krt_hbm, kvbuf, krbuf, sems, *, layer, n_pages, page, nb):
    i = pl.program_id(0)
    nsteps = pl.num_programs(0)
    slot = i % 2

    def page_copies(pg, j, p, sl):
        cols = pl.ds(p * page, page)
        return (pltpu.make_async_copy(ckv_hbm.at[layer, pg], kvbuf.at[sl, j, cols], sems.at[0, sl]),
                pltpu.make_async_copy(krt_hbm.at[layer, pg], krbuf.at[sl, j, :, cols], sems.at[1, sl]))

    def start_all(step, sl):
        for j in range(nb):
            for p in range(n_pages):
                for cp in page_copies(pt_ref[step * nb + j, p], j, p, sl):
                    cp.start()

    @pl.when(i == 0)
    def _():
        start_all(0, 0)

    @pl.when(i + 1 < nsteps)
    def _():
        start_all(i + 1, 1 - slot)

    for j in range(nb):
        for p in range(n_pages):
            for cp in page_copies(0, j, p, slot):
                cp.wait()
    return slot


def _sample_stages(q_ref, knew_ref, o_ref, kvbuf, krbuf, slot, *, dec_seq, nb):
    rows = q_ref.shape[1]
    tpos = lax.broadcasted_iota(jnp.int32, (rows, LANES), 0) & (dec_seq - 1)
    jpos = lax.broadcasted_iota(jnp.int32, (rows, LANES), 1)
    new_visible = jpos <= tpos
    kvs, s_pasts, s_news = [], [], []
    for j in range(nb):
        q = q_ref[j]
        kv = kvbuf[slot, j].astype(BF16)
        krt = krbuf[slot, j].astype(BF16)
        kvs.append(kv)
        s_pasts.append(_dot_nt(q[:, :KV_LORA], kv) + _dot(q[:, KV_LORA:], krt))
        s_news.append(jnp.where(new_visible, _dot_nt(q, knew_ref[j]), NEG_INF))
    yield
    probs = []
    for j in range(nb):
        m = jnp.maximum(jnp.max(s_pasts[j], axis=-1, keepdims=True),
                        jnp.max(s_news[j], axis=-1, keepdims=True))
        p_past = jnp.exp2(s_pasts[j] - m)
        p_new = jnp.exp2(s_news[j] - m)
        l = jnp.sum(p_past, axis=-1, keepdims=True) + jnp.sum(p_new, axis=-1, keepdims=True)
        probs.append((p_past.astype(BF16), p_new.astype(BF16), l))
    yield
    for j in range(nb):
        p_past, p_new, l = probs[j]
        acc = _dot(p_past, kvs[j]) + _dot(p_new, knew_ref[j][:, :KV_LORA])
        o_ref[j] = (acc / l).astype(BF16)


def _attn_sample_kernel(pt_ref, q_ref, knew_ref, ckv_hbm, krt_hbm, o_ref, kvbuf, krbuf, sems,
                        *, layer, n_pages, page, dec_seq, nb):
    slot = _sample_fetch(pt_ref, ckv_hbm, krt_hbm, kvbuf, krbuf, sems, layer=layer, n_pages=n_pages,
                         page=page, nb=nb)
    for _ in _sample_stages(q_ref, knew_ref, o_ref, kvbuf, krbuf, slot, dec_seq=dec_seq, nb=nb):
        pass


def _attn_sample(page_table, qs, knew, cache_kv, cache_krt, *, layer, nb):
    db, rows, _ = qs.shape
    n_pages = page_table.shape[1]
    page = cache_kv.shape[2]
    dec_seq = rows // N_HEADS
    grid_spec = pltpu.PrefetchScalarGridSpec(
        num_scalar_prefetch=1,
        grid=(db // nb,),
        in_specs=[pl.BlockSpec((nb, rows, KCAT), lambda i, pt: (i, 0, 0)),
                  pl.BlockSpec((nb, LANES, KCAT), lambda i, pt: (i, 0, 0)),
                  pl.BlockSpec(memory_space=pl.ANY),
                  pl.BlockSpec(memory_space=pl.ANY)],
        out_specs=pl.BlockSpec((nb, rows, KV_LORA), lambda i, pt: (i, 0, 0)),
        scratch_shapes=[pltpu.VMEM((2, nb, n_pages * page, KV_LORA), F32),
                        pltpu.VMEM((2, nb, ROPE_DIM, n_pages * page), F32),
                        pltpu.SemaphoreType.DMA((2, 2))],
    )
    return pl.pallas_call(
        functools.partial(_attn_sample_kernel, layer=layer, n_pages=n_pages, page=page,
                          dec_seq=dec_seq, nb=nb),
        grid_spec=grid_spec,
        out_shape=jax.ShapeDtypeStruct((db, rows, KV_LORA), BF16),
        compiler_params=pltpu.CompilerParams(dimension_semantics=("arbitrary",),
                                             vmem_limit_bytes=VMEM_LIMIT),
        name="attn_sample",
    )(page_table, qs, knew, cache_kv, cache_krt)


def _post_kernel(x_ref, ol_ref, g_ref, wo_ref, wg_ref, nmp_ref, nfp_ref, nfo_ref, wup_ref, wdn_ref,
                 y_ref, *, ff_chunk):
    mix = _dot(ol_ref[...], wo_ref[...]) + _dot(g_ref[...], wg_ref[...])
    x1 = x_ref[...] + _rms(mix, nmp_ref[...])
    h2 = _rms(x1, nfp_ref[...]).astype(BF16)
    d_ff = wup_ref.shape[1]
    f = None
    for c in range(d_ff // ff_chunk):
        cs = slice(c * ff_chunk, (c + 1) * ff_chunk)
        hf = _dot(h2, wup_ref[:, cs])
        act = jnp.square(jnp.maximum(hf, 0.0)).astype(BF16)
        part = _dot(act, wdn_ref[cs, :])
        f = part if f is None else f + part
    y_ref[...] = x1 + _rms(f, nfo_ref[...])


def _post(x, ol, gout, w, *, layer, tm, ff_chunk):
    n, d = x.shape

    def const(a):
        if a.ndim == 3:
            return pl.BlockSpec((None,) + a.shape[1:], lambda i: (layer, 0, 0),
                                pipeline_mode=pl.Buffered(1))
        return pl.BlockSpec(a.shape, lambda i: (0, 0), pipeline_mode=pl.Buffered(1))

    rows = lambda wdt: pl.BlockSpec((tm, wdt), lambda i: (i, 0))
    ws = [w["w_o"], w["w_out_g"], w["nm_post"], w["nf_pre"], w["nf_post"], w["w_up"], w["w_down"]]
    return pl.pallas_call(
        functools.partial(_post_kernel, ff_chunk=ff_chunk),
        grid=(n // tm,),
        in_specs=[rows(d), rows(QABS_W), rows(GM_WIDTH)] + [const(a) for a in ws],
        out_specs=rows(d),
        out_shape=jax.ShapeDtypeStruct((n, d), F32),
        compiler_params=pltpu.CompilerParams(dimension_semantics=("parallel",),
                                             vmem_limit_bytes=VMEM_LIMIT),
        name="post",
    )(x, ol, gout, *ws)


def _rope_tables(pos):
    half = ROPE_DIM // 2
    freqs = ROPE_THETA ** (-np.arange(half, dtype=np.float64) / half)
    ang = pos.astype(np.float64)[:, None] * freqs[None, :]
    tile = lambda t: jnp.asarray(np.tile(t.astype(np.float32), (1, 2 * N_HEADS)))
    return tile(np.cos(ang)), tile(np.sin(ang))


def _swap_halves(w):
    half = w.shape[-1] // 2
    return jnp.concatenate([-w[..., half:], w[..., :half]], axis=-1)


def _pair_groups(g):
    n = g.shape[0]
    return g.reshape(n // 2, 2, CHUNK, CHUNK).transpose(0, 2, 1, 3).reshape(n // 2, CHUNK, 2 * CHUNK)


def kernel(x_prompt, x_sample, cache_kv_latent, cache_k_rope, page_table, norm_mix_pre, norm_mix_post,
           w_in, q_norm, w_uq, kv_norm, w_uk, w_uv, v_norm, w_spatial, b_spatial, w_out, norm_ffn_pre,
           norm_ffn_post, w_ff_up, w_ff_down):
    batch, seq, d_model = x_prompt.shape
    dec_batch, dec_seq, _ = x_sample.shape
    depth = w_in.shape[0]
    n_pages, page = page_table.shape[1], cache_kv_latent.shape[2]
    past_len = n_pages * page
    tm = 512
    attn_blk = 256
    attn_kblk = 512
    attn_groups = 4
    attn_rchunk = 32
    ff_chunk = 1024
    sample_nb = 2

    cos_p, sin_p = _rope_tables(np.arange(seq))
    cos_s, sin_s = _rope_tables(past_len + np.arange(dec_batch * dec_seq) % dec_seq)
    cache_krt = jnp.swapaxes(cache_k_rope, 2, 3)

    w_abs, w_o = _prep(w_uq, w_uk, w_uv, w_out)
    w_out_g = w_out[:, N_HEADS * V_DIM:].astype(BF16)
    w_up = w_ff_up.astype(BF16)
    w_down = w_ff_down.astype(BF16)

    xp = x_prompt.reshape(batch * seq, d_model)
    xs = x_sample.reshape(dec_batch * dec_seq, d_model)
    ckv_p, kpe_p, ckv_s, kpe_s, v_s = [], [], [], [], []
    row = lambda a: a.reshape(1, -1)
    for l in range(depth):
        wi = w_in[l]
        kpe_w = wi[:, Q_LORA + KV_LORA:Q_LORA + KV_LORA + ROPE_DIM]
        pad = jnp.zeros((d_model, LANES - 2 * ROPE_DIM), F32)
        w_in_r = jnp.concatenate(
            [wi[:, :Q_LORA + KV_LORA], wi[:, Q_LORA + KV_LORA + ROPE_DIM:], kpe_w,
             _swap_halves(kpe_w), pad], axis=1).astype(BF16)
        wq_pe = w_uq[l][:, :, NOPE_DIM:]
        w_q = jnp.concatenate(
            [w_abs[l], wq_pe.reshape(Q_LORA, QPE_W).astype(BF16),
             _swap_halves(wq_pe).reshape(Q_LORA, QPE_W).astype(BF16)], axis=1)
        w = dict(nm_pre=row(norm_mix_pre[l]), w_in=w_in_r, q_norm=row(q_norm[l]),
                 kv_norm=row(kv_norm[l]), v_norm=row(v_norm[l]), w_q=w_q,
                 w_o=w_o, w_out_g=w_out_g,
                 nm_post=row(norm_mix_post[l]), nf_pre=row(norm_ffn_pre[l]),
                 nf_post=row(norm_ffn_post[l]), w_up=w_up, w_down=w_down)
        ws, bs = w_spatial[l], b_spatial[l]

        gp = _pair_groups(ws)
        gb = jnp.repeat(bs.T, GM_HEAD_DIM, axis=1)
        qst, kcat, ckv, kpe, gout = _proj(xp, w, cos_p, sin_p, gp, gb, period=CHUNK, emit_v=False,
                                          tm=tm, qblk=attn_blk)
        ckv_p.append(ckv.reshape(batch, seq // page, page, KV_LORA))
        kpe_p.append(kpe.reshape(batch, seq // page, page, ROPE_DIM))
        reps = CHUNK // dec_seq
        gp = _pair_groups(jnp.tile(jnp.tile(ws[:, :dec_seq, :dec_seq], (1, 1, reps)), (1, reps, 1)))
        gb = jnp.tile(jnp.repeat(bs[:, :dec_seq].T, GM_HEAD_DIM, axis=1), (reps, 1))
        qst_s, kcat_s, ckv, kpe, gout_s, vn = _proj(xs, w, cos_s, sin_s, gp, gb, period=dec_seq,
                                                    emit_v=True, tm=dec_batch * dec_seq, qblk=attn_blk)
        seqs_blk = attn_blk // dec_seq
        qs = qst_s.reshape(-1, N_HEADS, seqs_blk, dec_seq, KCAT).transpose(0, 2, 1, 3, 4).reshape(
            dec_batch, N_HEADS * dec_seq, KCAT)
        knew = jnp.pad(kcat_s.reshape(dec_batch, dec_seq, KCAT), ((0, 0), (0, LANES - dec_seq), (0, 0)))

        ol = _attn_prompt(qst, kcat, batch=batch, seq=seq, blk=attn_blk, kblk=attn_kblk,
                          groups=attn_groups, rchunk=attn_rchunk)
        xp = _post(xp, ol, gout, w, layer=l, tm=tm, ff_chunk=ff_chunk)
        o = _attn_sample(page_table, qs, knew, cache_kv_latent, cache_krt, layer=l, nb=sample_nb)
        ol_s = o.reshape(dec_batch, N_HEADS, dec_seq, KV_LORA).transpose(0, 2, 1, 3).reshape(
            dec_batch * dec_seq, QABS_W)
        xs = _post(xs, ol_s, gout_s, w, layer=l, tm=dec_batch * dec_seq, ff_chunk=ff_chunk)
        ckv_s.append(ckv.reshape(dec_batch, dec_seq, KV_LORA))
        kpe_s.append(kpe.reshape(dec_batch, dec_seq, ROPE_DIM))
        v_s.append(vn.reshape(dec_batch, dec_seq, GM_WIDTH))

    return (xp.reshape(batch, seq, d_model), xs.reshape(dec_batch, dec_seq, d_model),
            jnp.stack(ckv_p), jnp.stack(kpe_p), jnp.stack(ckv_s), jnp.stack(kpe_s), jnp.stack(v_s))
```

```python
import functools

import jax
import jax.numpy as jnp
import numpy as np
from jax import lax
from jax.experimental import pallas as pl
from jax.experimental.pallas import tpu as pltpu

N_HEADS = 8
NOPE_DIM = 64
ROPE_DIM = 32
V_DIM = 64
Q_LORA = 256
KV_LORA = 128
GM_GROUPS = 8
GM_HEAD_DIM = 64
GM_WIDTH = GM_GROUPS * GM_HEAD_DIM
CHUNK = 128
ROPE_THETA = 10000.0
EPS = 1e-6
ATTN_SCALE = (NOPE_DIM + ROPE_DIM) ** -0.5
LOG2E = 1.4426950408889634
Q_SCALE = ATTN_SCALE * LOG2E
NEG_INF = -1e30

LANES = 128
KCAT = KV_LORA + ROPE_DIM
QABS_W = N_HEADS * KV_LORA
QPE_W = N_HEADS * ROPE_DIM

C_QLAT = 0
C_CKV = Q_LORA
C_U = C_CKV + KV_LORA
C_V = C_U + GM_WIDTH
C_KPE = C_V + GM_WIDTH
C_KPE_SW = C_KPE + ROPE_DIM
IN_W = C_KPE + LANES

VMEM_LIMIT = 56 * 1024 * 1024

F32 = jnp.float32
BF16 = jnp.bfloat16


def _rms(x, g):
    ms = jnp.mean(x * x, axis=-1, keepdims=True)
    return x * lax.rsqrt(ms + EPS) * g


def _dot(a, b):
    return jnp.dot(a, b, preferred_element_type=F32)


def _lane_tile(x, n):
    return jnp.concatenate([x] * n, axis=1)


def _dot_nt(a, b):
    return lax.dot_general(a, b, (((1,), (1,)), ((), ())), preferred_element_type=F32)


def _prep_kernel(wqn_ref, wuk_ref, wuv_ref, wouta_ref, wabs_ref, wo_ref):
    for h in range(N_HEADS):
        wabs = _dot(wqn_ref[0, h], wuk_ref[0, h]) * Q_SCALE
        wabs_ref[0, :, h * KV_LORA:(h + 1) * KV_LORA] = wabs.astype(BF16)
        wo = _dot(wuv_ref[0, h], wouta_ref[0, h])
        wo_ref[0, h * KV_LORA:(h + 1) * KV_LORA, :] = wo.astype(BF16)


def _prep(w_uq, w_uk, w_uv, w_out):
    depth, _, _, _ = w_uq.shape
    d_model = w_out.shape[-1]
    wqn = jnp.transpose(w_uq[..., :NOPE_DIM], (0, 2, 1, 3))
    wuk = jnp.transpose(w_uk, (0, 2, 3, 1))
    wuv = jnp.transpose(w_uv, (0, 2, 1, 3))
    wouta = w_out[:, :N_HEADS * V_DIM].reshape(depth, N_HEADS, V_DIM, d_model)
    blk = lambda shp: pl.BlockSpec((1,) + shp, lambda l: (l,) + (0,) * len(shp))
    return pl.pallas_call(
        _prep_kernel,
        grid=(depth,),
        in_specs=[blk((N_HEADS, Q_LORA, NOPE_DIM)), blk((N_HEADS, NOPE_DIM, KV_LORA)),
                  blk((N_HEADS, KV_LORA, V_DIM)), blk((N_HEADS, V_DIM, d_model))],
        out_specs=[blk((Q_LORA, QABS_W)), blk((QABS_W, d_model))],
        out_shape=[jax.ShapeDtypeStruct((depth, Q_LORA, QABS_W), BF16),
                   jax.ShapeDtypeStruct((depth, QABS_W, d_model), BF16)],
        name="prep_fold",
    )(wqn, wuk, wuv, wouta)


def _proj_kernel(x_ref, nm_ref, win_ref, qn_ref, kvn_ref, vnw_ref, wq_ref, cos_ref, sin_ref,
                 gp_ref, gb_ref,
                 qst_ref, kcat_ref, ckv_ref, kpe_ref, gout_ref, *vn_out, period, qblk, sub, stack_q):
    tm = x_ref.shape[0]
    row = lax.broadcasted_iota(jnp.int32, (CHUNK, 2 * CHUNK), 0)
    col = lax.broadcasted_iota(jnp.int32, (CHUNK, 2 * CHUNK), 1) & (CHUNK - 1)
    mask = col <= row
    if period < CHUNK:
        sh = period.bit_length() - 1
        mask = mask & ((row >> sh) == (col >> sh))
    left = lax.broadcasted_iota(jnp.int32, (CHUNK, LANES), 1) < GM_HEAD_DIM
    gpk = [jnp.where(mask, gp_ref[k], 0.0).astype(BF16) for k in range(GM_GROUPS // 2)]

    def rows_block(r0, nrows):
        ts = slice(r0, r0 + nrows)
        x = x_ref[ts, :]
        r_x = lax.rsqrt(jnp.mean(x * x, axis=-1, keepdims=True) + EPS)
        zr = _dot((x * nm_ref[...]).astype(BF16), win_ref[...])

        zq = zr[:, C_QLAT:C_QLAT + Q_LORA]
        r_q = lax.rsqrt(r_x * r_x * jnp.mean(zq * zq, axis=-1, keepdims=True) + EPS)
        q = _dot((zq * qn_ref[...]).astype(BF16), wq_ref[...]) * (r_x * r_q)
        z = zr * r_x
        cos = cos_ref[ts, :]
        sin = sin_ref[ts, :]
        qpe = (q[:, QABS_W:QABS_W + QPE_W] * cos + q[:, QABS_W + QPE_W:] * sin) * Q_SCALE
        if not stack_q:
            qst_ref[ts, :QABS_W] = q[:, :QABS_W].astype(BF16)
            qst_ref[ts, QABS_W:] = qpe.astype(BF16)
        for b in range(nrows // qblk if stack_q else 0):
            rs = slice(b * qblk, (b + 1) * qblk)
            for hd in range(N_HEADS):
                hs = slice(hd * qblk, (hd + 1) * qblk)
                qb = r0 // qblk + b
                qst_ref[qb, hs, :KV_LORA] = q[rs, hd * KV_LORA:(hd + 1) * KV_LORA].astype(BF16)
                qst_ref[qb, hs, KV_LORA:] = qpe[rs, hd * ROPE_DIM:(hd + 1) * ROPE_DIM].astype(BF16)

        ckv = _rms(z[:, C_CKV:C_CKV + KV_LORA], kvn_ref[...])
        ckv_ref[ts, :] = ckv
        kpe = (z[:, C_KPE:C_KPE + ROPE_DIM] * cos[:, :ROPE_DIM]
               + z[:, C_KPE_SW:C_KPE_SW + ROPE_DIM] * sin[:, :ROPE_DIM])
        kpe_ref[ts, :] = kpe
        kcat_ref[ts, :KV_LORA] = ckv.astype(BF16)
        kcat_ref[ts, KV_LORA:] = kpe.astype(BF16)

        vn = _rms(z[:, C_V:C_V + GM_WIDTH], vnw_ref[...])
        if vn_out:
            vn_out[0][ts, :] = vn
        for k in range(GM_GROUPS // 2):
            cs = slice(k * LANES, (k + 1) * LANES)
            for c in range(nrows // CHUNK):
                rs = slice(c * CHUNK, (c + 1) * CHUNK)
                vcol = vn[rs, cs]
                rhs = jnp.concatenate([jnp.where(left, vcol, 0.0), jnp.where(left, 0.0, vcol)],
                                      axis=0).astype(BF16)
                mixed = _dot(gpk[k], rhs) + gb_ref[:, cs]
                gout_ref[r0 + c * CHUNK:r0 + (c + 1) * CHUNK, cs] = (
                    z[rs, C_U + k * LANES:C_U + (k + 1) * LANES] * mixed).astype(BF16)

    for r0 in range(0, tm, sub):
        rows_block(r0, sub)


def _proj(x, w, cos, sin, gp, gb, *, period, emit_v, tm, qblk, stack_q):
    n, d = x.shape
    n_pos_tiles = cos.shape[0] // tm
    const = lambda a: pl.BlockSpec(a.shape, lambda i: (0,) * a.ndim)
    rows = lambda wdt: pl.BlockSpec((tm, wdt), lambda i: (i, 0))
    pos = pl.BlockSpec((tm, QPE_W), lambda i: (i % n_pos_tiles, 0))
    out_w = [(KCAT, BF16), (KV_LORA, F32), (ROPE_DIM, F32), (GM_WIDTH, BF16)]
    if emit_v:
        out_w.append((GM_WIDTH, F32))
    ins = [x, w["nm_pre"], w["w_in"], w["q_norm"], w["kv_norm"], w["v_norm"], w["w_q"]]
    qst_rows = N_HEADS * qblk
    if stack_q:
        q_spec = pl.BlockSpec((tm // qblk, qst_rows, KCAT), lambda i: (i, 0, 0))
        q_shape = jax.ShapeDtypeStruct((n // qblk, qst_rows, KCAT), BF16)
    else:
        q_spec = rows(QABS_W + QPE_W)
        q_shape = jax.ShapeDtypeStruct((n, QABS_W + QPE_W), BF16)
    return pl.pallas_call(
        functools.partial(_proj_kernel, period=period, qblk=qblk, sub=qblk, stack_q=stack_q),
        grid=(n // tm,),
        in_specs=[rows(d)] + [const(a) for a in ins[1:]] + [pos, pos, const(gp), const(gb)],
        out_specs=[q_spec] + [rows(wd) for wd, _ in out_w],
        out_shape=[q_shape] + [jax.ShapeDtypeStruct((n, wd), dt) for wd, dt in out_w],
        compiler_params=pltpu.CompilerParams(dimension_semantics=("parallel",),
                                             vmem_limit_bytes=VMEM_LIMIT),
        name="proj",
    )(*ins, cos, sin, gp, gb)


def _attn_prompt_kernel(q_ref, k_ref, o_ref, s_scr, p_scr, m_scr, a_scr, acc_scr,
                        *, blk, kblk, groups, rchunk):
    gheads = N_HEADS // groups
    grows = gheads * blk
    qb = pl.program_id(1)
    m_scr[...] = jnp.full(m_scr.shape, NEG_INF, F32)
    acc_scr[...] = jnp.zeros(acc_scr.shape, F32)
    n_full = (qb * blk) // kblk

    def max_rows(r0, s, k0, masked):
        rows = slice(r0, r0 + rchunk)
        width = s.shape[1]
        if masked:
            qpos = qb * blk + ((r0 + lax.broadcasted_iota(jnp.int32, s.shape, 0)) & (blk - 1))
            kpos = k0 + lax.broadcasted_iota(jnp.int32, s.shape, 1)
            s = jnp.where(kpos <= qpos, s, NEG_INF)
        s_scr[rows, :width] = s
        m_old = m_scr[rows, :]
        m_new = jnp.maximum(m_old, jnp.max(s, axis=-1, keepdims=True))
        a_scr[rows, :] = jnp.exp2(m_old - m_new)
        m_scr[rows, :] = m_new

    def prob_rows(r0, width):
        rows = slice(r0, r0 + rchunk)
        x = s_scr[rows, :width] - _lane_tile(m_scr[rows, :], width // LANES)
        p_scr[rows, :width] = jnp.exp2(x.astype(BF16))

    def key_block(k0, width, masked):
        kt = k_ref[pl.ds(pl.multiple_of(k0, width), width), :]
        vext = jnp.concatenate([kt[:, :KV_LORA], jnp.ones((width, LANES), BF16)], axis=1)
        for g in range(groups):
            grp = slice(g * grows, (g + 1) * grows)
            s = _dot_nt(q_ref[grp, :], kt)
            for c in range(grows // rchunk):
                max_rows(g * grows + c * rchunk, s[c * rchunk:(c + 1) * rchunk, :], k0, masked)
        for g in range(groups):
            grp = slice(g * grows, (g + 1) * grows)
            for c in range(grows // rchunk):
                prob_rows(g * grows + c * rchunk, width)
            acc_scr[grp, :] = (_lane_tile(a_scr[grp, :], 2) * acc_scr[grp, :]
                               + _dot(p_scr[grp, :width], vext))

    def body(kb, carry):
        key_block(kb * kblk, kblk, False)
        return carry

    lax.fori_loop(0, n_full, body, 0)
    k_tail = n_full * kblk
    aligned = k_tail == qb * blk

    @pl.when(aligned)
    def _():
        key_block(k_tail, blk, True)

    @pl.when(jnp.logical_not(aligned))
    def _():
        key_block(k_tail, kblk, True)

    o = (acc_scr[:, :KV_LORA] / acc_scr[:, KV_LORA:]).astype(BF16)
    for h in range(N_HEADS):
        o_ref[:, h * KV_LORA:(h + 1) * KV_LORA] = o[h * blk:(h + 1) * blk, :]


def _attn_prompt(qst, kcat, *, batch, seq, blk, kblk, groups, rchunk):
    nq = seq // blk
    rows_all = N_HEADS * blk
    return pl.pallas_call(
        functools.partial(_attn_prompt_kernel, blk=blk, kblk=kblk, groups=groups, rchunk=rchunk),
        grid=(batch, nq),
        in_specs=[pl.BlockSpec((None, rows_all, KCAT), lambda b, i: (b * nq + i, 0, 0)),
                  pl.BlockSpec((seq, KCAT), lambda b, i: (b, 0))],
        out_specs=pl.BlockSpec((blk, QABS_W), lambda b, i: (b * nq + i, 0)),
        out_shape=jax.ShapeDtypeStruct((batch * seq, QABS_W), BF16),
        scratch_shapes=[pltpu.VMEM((rows_all, kblk), F32),
                        pltpu.VMEM((rows_all, kblk), BF16),
                        pltpu.VMEM((rows_all, LANES), F32),
                        pltpu.VMEM((rows_all, LANES), F32),
                        pltpu.VMEM((rows_all, KV_LORA + LANES), F32)],
        compiler_params=pltpu.CompilerParams(dimension_semantics=("parallel", "parallel"),
                                             vmem_limit_bytes=VMEM_LIMIT),
        name="attn_prompt",
    )(qst, kcat)


def _sample_fetch(pt_ref, ckv_hbm, krt_hbm, kvbuf, krbuf, sems, *, layer, n_pages, page, nb):
    i = pl.program_id(0)
    nsteps = pl.num_programs(0)
    slot = i % 2

    def page_copies(pg, j, p, sl):
        cols = pl.ds(p * page, page)
        return (pltpu.make_async_copy(ckv_hbm.at[layer, pg], kvbuf.at[sl, j, cols], sems.at[0, sl]),
                pltpu.make_async_copy(krt_hbm.at[layer, pg], krbuf.at[sl, j, :, cols], sems.at[1, sl]))

    def start_all(step, sl):
        for j in range(nb):
            for p in range(n_pages):
                for cp in page_copies(pt_ref[step * nb + j, p], j, p, sl):
                    cp.start()

    def wait_all(sl):
        for j in range(nb):
            for p in range(n_pages):
                for cp in page_copies(0, j, p, sl):
                    cp.wait()

    @pl.when(i == 0)
    def _():
        start_all(0, 0)

    wait_all(slot)
    start_all(jnp.minimum(i + 1, nsteps - 1), 1 - slot)

    def drain():
        @pl.when(i == nsteps - 1)
        def _():
            wait_all(1 - slot)

    return slot, drain


def _sample_stages(q_ref, knew_ref, o_ref, kvbuf, krbuf, slot, *, dec_seq, nb):
    rows = q_ref.shape[1]
    tpos = lax.broadcasted_iota(jnp.int32, (rows, LANES), 0) & (dec_seq - 1)
    jpos = lax.broadcasted_iota(jnp.int32, (rows, LANES), 1)
    new_visible = jpos <= tpos
    kvs, s_pasts, s_news = [], [], []
    for j in range(nb):
        q = q_ref[j]
        kv = kvbuf[slot, j].astype(BF16)
        krt = krbuf[slot, j].astype(BF16)
        kvs.append(kv)
        s_pasts.append(_dot_nt(q[:, :KV_LORA], kv) + _dot(q[:, KV_LORA:], krt))
        s_news.append(jnp.where(new_visible, _dot_nt(q, knew_ref[j]), NEG_INF))
    yield
    probs = []
    for j in range(nb):
        m = jnp.maximum(jnp.max(s_pasts[j], axis=-1, keepdims=True),
                        jnp.max(s_news[j], axis=-1, keepdims=True))
        p_past = jnp.exp2(s_pasts[j] - m)
        p_new = jnp.exp2(s_news[j] - m)
        l = jnp.sum(p_past, axis=-1, keepdims=True) + jnp.sum(p_new, axis=-1, keepdims=True)
        probs.append((p_past.astype(BF16), p_new.astype(BF16), l))
    yield
    for j in range(nb):
        p_past, p_new, l = probs[j]
        acc = _dot(p_past, kvs[j]) + _dot(p_new, knew_ref[j][:, :KV_LORA])
        o_ref[j] = (acc / l).astype(BF16)


def _attn_sample_kernel(pt_ref, q_ref, knew_ref, ckv_hbm, krt_hbm, o_ref, kvbuf, krbuf, sems,
                        *, layer, n_pages, page, dec_seq, nb):
    slot, drain = _sample_fetch(pt_ref, ckv_hbm, krt_hbm, kvbuf, krbuf, sems, layer=layer,
                                n_pages=n_pages, page=page, nb=nb)
    for _ in _sample_stages(q_ref, knew_ref, o_ref, kvbuf, krbuf, slot, dec_seq=dec_seq, nb=nb):
        pass
    drain()


def _attn_sample(page_table, qs, knew, cache_kv, cache_krt, *, layer, nb):
    db, rows, _ = qs.shape
    n_pages = page_table.shape[1]
    page = cache_kv.shape[2]
    dec_seq = rows // N_HEADS
    grid_spec = pltpu.PrefetchScalarGridSpec(
        num_scalar_prefetch=1,
        grid=(db // nb,),
        in_specs=[pl.BlockSpec((nb, rows, KCAT), lambda i, pt: (i, 0, 0)),
                  pl.BlockSpec((nb, LANES, KCAT), lambda i, pt: (i, 0, 0)),
                  pl.BlockSpec(memory_space=pl.ANY),
                  pl.BlockSpec(memory_space=pl.ANY)],
        out_specs=pl.BlockSpec((nb, rows, KV_LORA), lambda i, pt: (i, 0, 0)),
        scratch_shapes=[pltpu.VMEM((2, nb, n_pages * page, KV_LORA), F32),
                        pltpu.VMEM((2, nb, ROPE_DIM, n_pages * page), F32),
                        pltpu.SemaphoreType.DMA((2, 2))],
    )
    return pl.pallas_call(
        functools.partial(_attn_sample_kernel, layer=layer, n_pages=n_pages, page=page,
                          dec_seq=dec_seq, nb=nb),
        grid_spec=grid_spec,
        out_shape=jax.ShapeDtypeStruct((db, rows, KV_LORA), BF16),
        compiler_params=pltpu.CompilerParams(dimension_semantics=("arbitrary",),
                                             vmem_limit_bytes=VMEM_LIMIT),
        name="attn_sample",
    )(page_table, qs, knew, cache_kv, cache_krt)


def _post_kernel(x_ref, ol_ref, g_ref, wo_ref, wg_ref, nmp_ref, nfp_ref, nfo_ref, wup_ref, wdn_ref,
                 y_ref, *, ff_chunk):
    mix = _dot(ol_ref[...], wo_ref[...]) + _dot(g_ref[...], wg_ref[...])
    x1 = x_ref[...] + _rms(mix, nmp_ref[...])
    h2 = _rms(x1, nfp_ref[...]).astype(BF16)
    d_ff = wup_ref.shape[1]
    f = None
    for c in range(d_ff // ff_chunk):
        cs = slice(c * ff_chunk, (c + 1) * ff_chunk)
        hf = _dot(h2, wup_ref[:, cs])
        act = jnp.square(jnp.maximum(hf, 0.0)).astype(BF16)
        part = _dot(act, wdn_ref[cs, :])
        f = part if f is None else f + part
    y_ref[...] = x1 + _rms(f, nfo_ref[...])


def _post(x, ol, gout, w, *, layer, tm, ff_chunk):
    n, d = x.shape

    def const(a):
        if a.ndim == 3:
            return pl.BlockSpec((None,) + a.shape[1:], lambda i: (layer, 0, 0),
                                pipeline_mode=pl.Buffered(1))
        return pl.BlockSpec(a.shape, lambda i: (0, 0), pipeline_mode=pl.Buffered(1))

    rows = lambda wdt: pl.BlockSpec((tm, wdt), lambda i: (i, 0))
    ws = [w["w_o"], w["w_out_g"], w["nm_post"], w["nf_pre"], w["nf_post"], w["w_up"], w["w_down"]]
    return pl.pallas_call(
        functools.partial(_post_kernel, ff_chunk=ff_chunk),
        grid=(n // tm,),
        in_specs=[rows(d), rows(QABS_W), rows(GM_WIDTH)] + [const(a) for a in ws],
        out_specs=rows(d),
        out_shape=jax.ShapeDtypeStruct((n, d), F32),
        compiler_params=pltpu.CompilerParams(dimension_semantics=("parallel",),
                                             vmem_limit_bytes=VMEM_LIMIT),
        name="post",
    )(x, ol, gout, *ws)


def _rope_tables(pos):
    half = ROPE_DIM // 2
    freqs = ROPE_THETA ** (-np.arange(half, dtype=np.float64) / half)
    ang = pos.astype(np.float64)[:, None] * freqs[None, :]
    tile = lambda t: jnp.asarray(np.tile(t.astype(np.float32), (1, 2 * N_HEADS)))
    return tile(np.cos(ang)), tile(np.sin(ang))


def _swap_halves(w):
    half = w.shape[-1] // 2
    return jnp.concatenate([-w[..., half:], w[..., :half]], axis=-1)


def _pair_groups(g):
    n = g.shape[0]
    return g.reshape(n // 2, 2, CHUNK, CHUNK).transpose(0, 2, 1, 3).reshape(n // 2, CHUNK, 2 * CHUNK)


def kernel(x_prompt, x_sample, cache_kv_latent, cache_k_rope, page_table, norm_mix_pre, norm_mix_post,
           w_in, q_norm, w_uq, kv_norm, w_uk, w_uv, v_norm, w_spatial, b_spatial, w_out, norm_ffn_pre,
           norm_ffn_post, w_ff_up, w_ff_down):
    batch, seq, d_model = x_prompt.shape
    dec_batch, dec_seq, _ = x_sample.shape
    depth = w_in.shape[0]
    n_pages, page = page_table.shape[1], cache_kv_latent.shape[2]
    past_len = n_pages * page
    tm = 512
    attn_blk = 256
    attn_kblk = 512
    attn_groups = 4
    attn_rchunk = 32
    ff_chunk = 1024
    sample_nb = 2

    cos_p, sin_p = _rope_tables(np.arange(seq))
    cos_s, sin_s = _rope_tables(past_len + np.arange(dec_batch * dec_seq) % dec_seq)
    cache_krt = jnp.swapaxes(cache_k_rope, 2, 3)

    w_abs, w_o = _prep(w_uq, w_uk, w_uv, w_out)
    w_out_g = w_out[:, N_HEADS * V_DIM:].astype(BF16)
    w_up = w_ff_up.astype(BF16)
    w_down = w_ff_down.astype(BF16)

    xp = x_prompt.reshape(batch * seq, d_model)
    xs = x_sample.reshape(dec_batch * dec_seq, d_model)
    ckv_p, kpe_p, ckv_s, kpe_s, v_s = [], [], [], [], []
    row = lambda a: a.reshape(1, -1)
    for l in range(depth):
        wi = w_in[l]
        kpe_w = wi[:, Q_LORA + KV_LORA:Q_LORA + KV_LORA + ROPE_DIM]
        pad = jnp.zeros((d_model, LANES - 2 * ROPE_DIM), F32)
        w_in_r = jnp.concatenate(
            [wi[:, :Q_LORA + KV_LORA], wi[:, Q_LORA + KV_LORA + ROPE_DIM:], kpe_w,
             _swap_halves(kpe_w), pad], axis=1).astype(BF16)
        wq_pe = w_uq[l][:, :, NOPE_DIM:]
        w_q = jnp.concatenate(
            [w_abs[l], wq_pe.reshape(Q_LORA, QPE_W).astype(BF16),
             _swap_halves(wq_pe).reshape(Q_LORA, QPE_W).astype(BF16)], axis=1)
        w = dict(nm_pre=row(norm_mix_pre[l]), w_in=w_in_r, q_norm=row(q_norm[l]),
                 kv_norm=row(kv_norm[l]), v_norm=row(v_norm[l]), w_q=w_q,
                 w_o=w_o, w_out_g=w_out_g,
                 nm_post=row(norm_mix_post[l]), nf_pre=row(norm_ffn_pre[l]),
                 nf_post=row(norm_ffn_post[l]), w_up=w_up, w_down=w_down)
        ws, bs = w_spatial[l], b_spatial[l]

        gp = _pair_groups(ws)
        gb = jnp.repeat(bs.T, GM_HEAD_DIM, axis=1)
        qst, kcat, ckv, kpe, gout = _proj(xp, w, cos_p, sin_p, gp, gb, period=CHUNK, emit_v=False,
                                          tm=tm, qblk=attn_blk, stack_q=True)
        ckv_p.append(ckv.reshape(batch, seq // page, page, KV_LORA))
        kpe_p.append(kpe.reshape(batch, seq // page, page, ROPE_DIM))
        reps = CHUNK // dec_seq
        gp = _pair_groups(jnp.tile(jnp.tile(ws[:, :dec_seq, :dec_seq], (1, 1, reps)), (1, reps, 1)))
        gb = jnp.tile(jnp.repeat(bs[:, :dec_seq].T, GM_HEAD_DIM, axis=1), (reps, 1))
        q_s, kcat_s, ckv, kpe, gout_s, vn = _proj(xs, w, cos_s, sin_s, gp, gb, period=dec_seq,
                                                  emit_v=True, tm=dec_batch * dec_seq, qblk=attn_blk,
                                                  stack_q=False)
        heads_to_rows = lambda a, wd: a.reshape(dec_batch, dec_seq, N_HEADS, wd).transpose(
            0, 2, 1, 3).reshape(dec_batch, N_HEADS * dec_seq, wd)
        qs = jnp.concatenate([heads_to_rows(q_s[:, :QABS_W], KV_LORA),
                              heads_to_rows(q_s[:, QABS_W:], ROPE_DIM)], axis=-1)
        knew = jnp.pad(kcat_s.reshape(dec_batch, dec_seq, KCAT), ((0, 0), (0, LANES - dec_seq), (0, 0)))

        ol = _attn_prompt(qst, kcat, batch=batch, seq=seq, blk=attn_blk, kblk=attn_kblk,
                          groups=attn_groups, rchunk=attn_rchunk)
        xp = _post(xp, ol, gout, w, layer=l, tm=tm, ff_chunk=ff_chunk)
        o = _attn_sample(page_table, qs, knew, cache_kv_latent, cache_krt, layer=l, nb=sample_nb)
        ol_s = o.reshape(dec_batch, N_HEADS, dec_seq, KV_LORA).transpose(0, 2, 1, 3).reshape(
            dec_batch * dec_seq, QABS_W)
        xs = _post(xs, ol_s, gout_s, w, layer=l, tm=dec_batch * dec_seq, ff_chunk=ff_chunk)
        ckv_s.append(ckv.reshape(dec_batch, dec_seq, KV_LORA))
        kpe_s.append(kpe.reshape(dec_batch, dec_seq, ROPE_DIM))
        v_s.append(vn.reshape(dec_batch, dec_seq, GM_WIDTH))

    return (xp.reshape(batch, seq, d_model), xs.reshape(dec_batch, dec_seq, d_model),
            jnp.stack(ckv_p), jnp.stack(kpe_p), jnp.stack(ckv_s), jnp.stack(kpe_s), jnp.stack(v_s))
```

```python
import functools

import jax
import jax.numpy as jnp
import numpy as np
from jax import lax
from jax.experimental import pallas as pl
from jax.experimental.pallas import tpu as pltpu

N_HEADS = 8
NOPE_DIM = 64
ROPE_DIM = 32
V_DIM = 64
Q_LORA = 256
KV_LORA = 128
GM_GROUPS = 8
GM_HEAD_DIM = 64
GM_WIDTH = GM_GROUPS * GM_HEAD_DIM
CHUNK = 128
ROPE_THETA = 10000.0
EPS = 1e-6
ATTN_SCALE = (NOPE_DIM + ROPE_DIM) ** -0.5
LOG2E = 1.4426950408889634
Q_SCALE = ATTN_SCALE * LOG2E
NEG_INF = -1e30

LANES = 128
KCAT = KV_LORA + ROPE_DIM
QABS_W = N_HEADS * KV_LORA
QPE_W = N_HEADS * ROPE_DIM

C_QLAT = 0
C_CKV = Q_LORA
C_U = C_CKV + KV_LORA
C_V = C_U + GM_WIDTH
C_KPE = C_V + GM_WIDTH
C_KPE_SW = C_KPE + ROPE_DIM
IN_W = C_KPE + LANES

VMEM_LIMIT = 56 * 1024 * 1024

F32 = jnp.float32
BF16 = jnp.bfloat16


def _rms(x, g):
    ms = jnp.mean(x * x, axis=-1, keepdims=True)
    return x * lax.rsqrt(ms + EPS) * g


def _dot(a, b):
    return jnp.dot(a, b, preferred_element_type=F32)


def _lane_tile(x, n):
    return jnp.concatenate([x] * n, axis=1)


def _dot_nt(a, b):
    return lax.dot_general(a, b, (((1,), (1,)), ((), ())), preferred_element_type=F32)


def _prep_kernel(wqn_ref, wuk_ref, wuv_ref, wouta_ref, wabs_ref, wo_ref):
    for h in range(N_HEADS):
        wabs = _dot(wqn_ref[0, h], wuk_ref[0, h]) * Q_SCALE
        wabs_ref[0, :, h * KV_LORA:(h + 1) * KV_LORA] = wabs.astype(BF16)
        wo = _dot(wuv_ref[0, h], wouta_ref[0, h])
        wo_ref[0, h * KV_LORA:(h + 1) * KV_LORA, :] = wo.astype(BF16)


def _prep(w_uq, w_uk, w_uv, w_out):
    depth, _, _, _ = w_uq.shape
    d_model = w_out.shape[-1]
    wqn = jnp.transpose(w_uq[..., :NOPE_DIM], (0, 2, 1, 3))
    wuk = jnp.transpose(w_uk, (0, 2, 3, 1))
    wuv = jnp.transpose(w_uv, (0, 2, 1, 3))
    wouta = w_out[:, :N_HEADS * V_DIM].reshape(depth, N_HEADS, V_DIM, d_model)
    blk = lambda shp: pl.BlockSpec((1,) + shp, lambda l: (l,) + (0,) * len(shp))
    return pl.pallas_call(
        _prep_kernel,
        grid=(depth,),
        in_specs=[blk((N_HEADS, Q_LORA, NOPE_DIM)), blk((N_HEADS, NOPE_DIM, KV_LORA)),
                  blk((N_HEADS, KV_LORA, V_DIM)), blk((N_HEADS, V_DIM, d_model))],
        out_specs=[blk((Q_LORA, QABS_W)), blk((QABS_W, d_model))],
        out_shape=[jax.ShapeDtypeStruct((depth, Q_LORA, QABS_W), BF16),
                   jax.ShapeDtypeStruct((depth, QABS_W, d_model), BF16)],
        name="prep_fold",
    )(wqn, wuk, wuv, wouta)


def _proj_kernel(x_ref, nm_ref, win_ref, qn_ref, kvn_ref, vnw_ref, wq_ref, cos_ref, sin_ref,
                 gp_ref, gb_ref,
                 qst_ref, kcat_ref, ckv_ref, kpe_ref, gout_ref, *vn_out, period, qblk, sub, stack_q):
    tm = x_ref.shape[0]
    row = lax.broadcasted_iota(jnp.int32, (CHUNK, 2 * CHUNK), 0)
    col = lax.broadcasted_iota(jnp.int32, (CHUNK, 2 * CHUNK), 1) & (CHUNK - 1)
    mask = col <= row
    if period < CHUNK:
        sh = period.bit_length() - 1
        mask = mask & ((row >> sh) == (col >> sh))
    left = lax.broadcasted_iota(jnp.int32, (CHUNK, LANES), 1) < GM_HEAD_DIM
    gpk = [jnp.where(mask, gp_ref[k], 0.0).astype(BF16) for k in range(GM_GROUPS // 2)]

    def rows_block(r0, nrows):
        ts = slice(r0, r0 + nrows)
        x = x_ref[ts, :]
        r_x = lax.rsqrt(jnp.mean(x * x, axis=-1, keepdims=True) + EPS)
        zr = _dot((x * nm_ref[...]).astype(BF16), win_ref[...])

        zq = zr[:, C_QLAT:C_QLAT + Q_LORA]
        r_q = lax.rsqrt(r_x * r_x * jnp.mean(zq * zq, axis=-1, keepdims=True) + EPS)
        q = _dot((zq * qn_ref[...]).astype(BF16), wq_ref[...]) * (r_x * r_q)
        z = zr * r_x
        cos = cos_ref[ts, :]
        sin = sin_ref[ts, :]
        qpe = (q[:, QABS_W:QABS_W + QPE_W] * cos + q[:, QABS_W + QPE_W:] * sin) * Q_SCALE
        if not stack_q:
            qst_ref[ts, :QABS_W] = q[:, :QABS_W].astype(BF16)
            qst_ref[ts, QABS_W:] = qpe.astype(BF16)
        for b in range(nrows // qblk if stack_q else 0):
            rs = slice(b * qblk, (b + 1) * qblk)
            for hd in range(N_HEADS):
                hs = slice(hd * qblk, (hd + 1) * qblk)
                qb = r0 // qblk + b
                qst_ref[qb, hs, :KV_LORA] = q[rs, hd * KV_LORA:(hd + 1) * KV_LORA].astype(BF16)
                qst_ref[qb, hs, KV_LORA:] = qpe[rs, hd * ROPE_DIM:(hd + 1) * ROPE_DIM].astype(BF16)

        ckv = _rms(z[:, C_CKV:C_CKV + KV_LORA], kvn_ref[...])
        ckv_ref[ts, :] = ckv
        kpe = (z[:, C_KPE:C_KPE + ROPE_DIM] * cos[:, :ROPE_DIM]
               + z[:, C_KPE_SW:C_KPE_SW + ROPE_DIM] * sin[:, :ROPE_DIM])
        kpe_ref[ts, :] = kpe
        kcat_ref[ts, :KV_LORA] = ckv.astype(BF16)
        kcat_ref[ts, KV_LORA:] = kpe.astype(BF16)

        vn = _rms(z[:, C_V:C_V + GM_WIDTH], vnw_ref[...])
        if vn_out:
            vn_out[0][ts, :] = vn
        for k in range(GM_GROUPS // 2):
            cs = slice(k * LANES, (k + 1) * LANES)
            for c in range(nrows // CHUNK):
                rs = slice(c * CHUNK, (c + 1) * CHUNK)
                vcol = vn[rs, cs]
                rhs = jnp.concatenate([jnp.where(left, vcol, 0.0), jnp.where(left, 0.0, vcol)],
                                      axis=0).astype(BF16)
                mixed = _dot(gpk[k], rhs) + gb_ref[:, cs]
                gout_ref[r0 + c * CHUNK:r0 + (c + 1) * CHUNK, cs] = (
                    z[rs, C_U + k * LANES:C_U + (k + 1) * LANES] * mixed).astype(BF16)

    for r0 in range(0, tm, sub):
        rows_block(r0, sub)


def _proj(x, w, cos, sin, gp, gb, *, period, emit_v, tm, qblk, stack_q):
    n, d = x.shape
    n_pos_tiles = cos.shape[0] // tm
    const = lambda a: pl.BlockSpec(a.shape, lambda i: (0,) * a.ndim)
    rows = lambda wdt: pl.BlockSpec((tm, wdt), lambda i: (i, 0))
    pos = pl.BlockSpec((tm, QPE_W), lambda i: (i % n_pos_tiles, 0))
    out_w = [(KCAT, BF16), (KV_LORA, F32), (ROPE_DIM, F32), (GM_WIDTH, BF16)]
    if emit_v:
        out_w.append((GM_WIDTH, F32))
    ins = [x, w["nm_pre"], w["w_in"], w["q_norm"], w["kv_norm"], w["v_norm"], w["w_q"]]
    qst_rows = N_HEADS * qblk
    if stack_q:
        q_spec = pl.BlockSpec((tm // qblk, qst_rows, KCAT), lambda i: (i, 0, 0))
        q_shape = jax.ShapeDtypeStruct((n // qblk, qst_rows, KCAT), BF16)
    else:
        q_spec = rows(QABS_W + QPE_W)
        q_shape = jax.ShapeDtypeStruct((n, QABS_W + QPE_W), BF16)
    return pl.pallas_call(
        functools.partial(_proj_kernel, period=period, qblk=qblk, sub=qblk, stack_q=stack_q),
        grid=(n // tm,),
        in_specs=[rows(d)] + [const(a) for a in ins[1:]] + [pos, pos, const(gp), const(gb)],
        out_specs=[q_spec] + [rows(wd) for wd, _ in out_w],
        out_shape=[q_shape] + [jax.ShapeDtypeStruct((n, wd), dt) for wd, dt in out_w],
        compiler_params=pltpu.CompilerParams(dimension_semantics=("parallel",),
                                             vmem_limit_bytes=VMEM_LIMIT),
        name="proj",
    )(*ins, cos, sin, gp, gb)


def _attn_prompt_kernel(q_ref, k_ref, o_ref, s_scr, p_scr, m_scr, a_scr, acc_scr,
                        *, blk, kblk, groups, rchunk):
    gheads = N_HEADS // groups
    grows = gheads * blk
    qb = pl.program_id(1)
    m_scr[...] = jnp.full(m_scr.shape, NEG_INF, F32)
    acc_scr[...] = jnp.zeros(acc_scr.shape, F32)
    n_full = (qb * blk) // kblk

    def max_rows(r0, s, k0, masked):
        rows = slice(r0, r0 + rchunk)
        width = s.shape[1]
        if masked:
            qpos = qb * blk + ((r0 + lax.broadcasted_iota(jnp.int32, s.shape, 0)) & (blk - 1))
            kpos = k0 + lax.broadcasted_iota(jnp.int32, s.shape, 1)
            s = jnp.where(kpos <= qpos, s, NEG_INF)
        s_scr[rows, :width] = s
        m_old = m_scr[rows, :]
        m_new = jnp.maximum(m_old, jnp.max(s, axis=-1, keepdims=True))
        a_scr[rows, :] = jnp.exp2(m_old - m_new)
        m_scr[rows, :] = m_new

    def prob_rows(r0, width):
        rows = slice(r0, r0 + rchunk)
        x = s_scr[rows, :width] - _lane_tile(m_scr[rows, :], width // LANES)
        p_scr[rows, :width] = jnp.exp2(x.astype(BF16))

    def key_block(k0, width, masked):
        kt = k_ref[pl.ds(pl.multiple_of(k0, width), width), :]
        vext = jnp.concatenate([kt[:, :KV_LORA], jnp.ones((width, LANES), BF16)], axis=1)
        for g in range(groups):
            grp = slice(g * grows, (g + 1) * grows)
            s = _dot_nt(q_ref[grp, :], kt)
            for c in range(grows // rchunk):
                max_rows(g * grows + c * rchunk, s[c * rchunk:(c + 1) * rchunk, :], k0, masked)
        for g in range(groups):
            grp = slice(g * grows, (g + 1) * grows)
            for c in range(grows // rchunk):
                prob_rows(g * grows + c * rchunk, width)
            acc_scr[grp, :] = (_lane_tile(a_scr[grp, :], 2) * acc_scr[grp, :]
                               + _dot(p_scr[grp, :width], vext))

    def body(kb, carry):
        key_block(kb * kblk, kblk, False)
        return carry

    lax.fori_loop(0, n_full, body, 0)
    k_tail = n_full * kblk
    aligned = k_tail == qb * blk

    @pl.when(aligned)
    def _():
        key_block(k_tail, blk, True)

    @pl.when(jnp.logical_not(aligned))
    def _():
        key_block(k_tail, kblk, True)

    o = (acc_scr[:, :KV_LORA] / acc_scr[:, KV_LORA:]).astype(BF16)
    for h in range(N_HEADS):
        o_ref[:, h * KV_LORA:(h + 1) * KV_LORA] = o[h * blk:(h + 1) * blk, :]


def _attn_prompt(qst, kcat, *, batch, seq, blk, kblk, groups, rchunk):
    nq = seq // blk
    rows_all = N_HEADS * blk
    return pl.pallas_call(
        functools.partial(_attn_prompt_kernel, blk=blk, kblk=kblk, groups=groups, rchunk=rchunk),
        grid=(batch, nq),
        in_specs=[pl.BlockSpec((None, rows_all, KCAT), lambda b, i: (b * nq + i, 0, 0)),
                  pl.BlockSpec((seq, KCAT), lambda b, i: (b, 0))],
        out_specs=pl.BlockSpec((blk, QABS_W), lambda b, i: (b * nq + i, 0)),
        out_shape=jax.ShapeDtypeStruct((batch * seq, QABS_W), BF16),
        scratch_shapes=[pltpu.VMEM((rows_all, kblk), F32),
                        pltpu.VMEM((rows_all, kblk), BF16),
                        pltpu.VMEM((rows_all, LANES), F32),
                        pltpu.VMEM((rows_all, LANES), F32),
                        pltpu.VMEM((rows_all, KV_LORA + LANES), F32)],
        compiler_params=pltpu.CompilerParams(dimension_semantics=("parallel", "parallel"),
                                             vmem_limit_bytes=VMEM_LIMIT),
        name="attn_prompt",
    )(qst, kcat)


def _sample_fetch(pt_ref, ckv_hbm, krt_hbm, kvbuf, krbuf, sems, *, layer, n_pages, page, nb):
    i = pl.program_id(0)
    nsteps = pl.num_programs(0)
    slot = i % 2

    def page_copies(pg, j, p, sl):
        cols = pl.ds(p * page, page)
        return (pltpu.make_async_copy(ckv_hbm.at[layer, pg], kvbuf.at[sl, j, cols], sems.at[0, sl]),
                pltpu.make_async_copy(krt_hbm.at[layer, pg], krbuf.at[sl, j, :, cols], sems.at[1, sl]))

    def start_all(step, sl):
        for j in range(nb):
            for p in range(n_pages):
                for cp in page_copies(pt_ref[step * nb + j, p], j, p, sl):
                    cp.start(priority=p % 2)

    def wait_all(sl):
        for j in range(nb):
            for p in range(n_pages):
                for cp in page_copies(0, j, p, sl):
                    cp.wait()

    @pl.when(i == 0)
    def _():
        start_all(0, 0)

    @pl.when(i + 1 < nsteps)
    def _():
        start_all(i + 1, 1 - slot)

    wait_all(slot)
    return slot


def _sample_stages(q_ref, knew_ref, o_ref, kvbuf, krbuf, slot, *, dec_seq, nb):
    rows = q_ref.shape[1]
    tpos = lax.broadcasted_iota(jnp.int32, (rows, LANES), 0) & (dec_seq - 1)
    jpos = lax.broadcasted_iota(jnp.int32, (rows, LANES), 1)
    new_visible = jpos <= tpos
    kvs, s_pasts, s_news = [], [], []
    for j in range(nb):
        q = q_ref[j]
        kv = kvbuf[slot, j].astype(BF16)
        krt = krbuf[slot, j].astype(BF16)
        kvs.append(kv)
        s_pasts.append(_dot_nt(q[:, :KV_LORA], kv) + _dot(q[:, KV_LORA:], krt))
        s_news.append(jnp.where(new_visible, _dot_nt(q, knew_ref[j]), NEG_INF))
    yield
    probs = []
    for j in range(nb):
        m = jnp.maximum(jnp.max(s_pasts[j], axis=-1, keepdims=True),
                        jnp.max(s_news[j], axis=-1, keepdims=True))
        p_past = jnp.exp2(s_pasts[j] - m)
        p_new = jnp.exp2(s_news[j] - m)
        l = jnp.sum(p_past, axis=-1, keepdims=True) + jnp.sum(p_new, axis=-1, keepdims=True)
        probs.append((p_past.astype(BF16), p_new.astype(BF16), l))
    yield
    for j in range(nb):
        p_past, p_new, l = probs[j]
        acc = _dot(p_past, kvs[j]) + _dot(p_new, knew_ref[j][:, :KV_LORA])
        o_ref[j] = (acc / l).astype(BF16)


def _attn_sample_kernel(pt_ref, q_ref, knew_ref, ckv_hbm, krt_hbm, o_ref, kvbuf, krbuf, sems,
                        *, layer, n_pages, page, dec_seq, nb):
    slot = _sample_fetch(pt_ref, ckv_hbm, krt_hbm, kvbuf, krbuf, sems, layer=layer, n_pages=n_pages,
                         page=page, nb=nb)
    for _ in _sample_stages(q_ref, knew_ref, o_ref, kvbuf, krbuf, slot, dec_seq=dec_seq, nb=nb):
        pass


def _attn_sample(page_table, qs, knew, cache_kv, cache_krt, *, layer, nb):
    db, rows, _ = qs.shape
    n_pages = page_table.shape[1]
    page = cache_kv.shape[2]
    dec_seq = rows // N_HEADS
    grid_spec = pltpu.PrefetchScalarGridSpec(
        num_scalar_prefetch=1,
        grid=(db // nb,),
        in_specs=[pl.BlockSpec((nb, rows, KCAT), lambda i, pt: (i, 0, 0)),
                  pl.BlockSpec((nb, LANES, KCAT), lambda i, pt: (i, 0, 0)),
                  pl.BlockSpec(memory_space=pl.ANY),
                  pl.BlockSpec(memory_space=pl.ANY)],
        out_specs=pl.BlockSpec((nb, rows, KV_LORA), lambda i, pt: (i, 0, 0)),
        scratch_shapes=[pltpu.VMEM((2, nb, n_pages * page, KV_LORA), F32),
                        pltpu.VMEM((2, nb, ROPE_DIM, n_pages * page), F32),
                        pltpu.SemaphoreType.DMA((2, 2))],
    )
    return pl.pallas_call(
        functools.partial(_attn_sample_kernel, layer=layer, n_pages=n_pages, page=page,
                          dec_seq=dec_seq, nb=nb),
        grid_spec=grid_spec,
        out_shape=jax.ShapeDtypeStruct((db, rows, KV_LORA), BF16),
        compiler_params=pltpu.CompilerParams(dimension_semantics=("arbitrary",),
                                             vmem_limit_bytes=VMEM_LIMIT),
        name="attn_sample",
    )(page_table, qs, knew, cache_kv, cache_krt)


def _post_kernel(x_ref, ol_ref, g_ref, wo_ref, wg_ref, nmp_ref, nfp_ref, nfo_ref, wup_ref, wdn_ref,
                 y_ref, *, ff_chunk):
    mix = _dot(ol_ref[...], wo_ref[...]) + _dot(g_ref[...], wg_ref[...])
    x1 = x_ref[...] + _rms(mix, nmp_ref[...])
    h2 = _rms(x1, nfp_ref[...]).astype(BF16)
    d_ff = wup_ref.shape[1]
    f = None
    for c in range(d_ff // ff_chunk):
        cs = slice(c * ff_chunk, (c + 1) * ff_chunk)
        hf = _dot(h2, wup_ref[:, cs])
        act = jnp.square(jnp.maximum(hf, 0.0)).astype(BF16)
        part = _dot(act, wdn_ref[cs, :])
        f = part if f is None else f + part
    y_ref[...] = x1 + _rms(f, nfo_ref[...])


def _post(x, ol, gout, w, *, layer, tm, ff_chunk):
    n, d = x.shape

    def const(a):
        if a.ndim == 3:
            return pl.BlockSpec((None,) + a.shape[1:], lambda i: (layer, 0, 0),
                                pipeline_mode=pl.Buffered(1))
        return pl.BlockSpec(a.shape, lambda i: (0, 0), pipeline_mode=pl.Buffered(1))

    rows = lambda wdt: pl.BlockSpec((tm, wdt), lambda i: (i, 0))
    ws = [w["w_o"], w["w_out_g"], w["nm_post"], w["nf_pre"], w["nf_post"], w["w_up"], w["w_down"]]
    return pl.pallas_call(
        functools.partial(_post_kernel, ff_chunk=ff_chunk),
        grid=(n // tm,),
        in_specs=[rows(d), rows(QABS_W), rows(GM_WIDTH)] + [const(a) for a in ws],
        out_specs=rows(d),
        out_shape=jax.ShapeDtypeStruct((n, d), F32),
        compiler_params=pltpu.CompilerParams(dimension_semantics=("parallel",),
                                             vmem_limit_bytes=VMEM_LIMIT),
        name="post",
    )(x, ol, gout, *ws)


def _rope_tables(pos):
    half = ROPE_DIM // 2
    freqs = ROPE_THETA ** (-np.arange(half, dtype=np.float64) / half)
    ang = pos.astype(np.float64)[:, None] * freqs[None, :]
    tile = lambda t: jnp.asarray(np.tile(t.astype(np.float32), (1, 2 * N_HEADS)))
    return tile(np.cos(ang)), tile(np.sin(ang))


def _swap_halves(w):
    half = w.shape[-1] // 2
    return jnp.concatenate([-w[..., half:], w[..., :half]], axis=-1)


def _pair_groups(g):
    n = g.shape[0]
    return g.reshape(n // 2, 2, CHUNK, CHUNK).transpose(0, 2, 1, 3).reshape(n // 2, CHUNK, 2 * CHUNK)


def kernel(x_prompt, x_sample, cache_kv_latent, cache_k_rope, page_table, norm_mix_pre, norm_mix_post,
           w_in, q_norm, w_uq, kv_norm, w_uk, w_uv, v_norm, w_spatial, b_spatial, w_out, norm_ffn_pre,
           norm_ffn_post, w_ff_up, w_ff_down):
    batch, seq, d_model = x_prompt.shape
    dec_batch, dec_seq, _ = x_sample.shape
    depth = w_in.shape[0]
    n_pages, page = page_table.shape[1], cache_kv_latent.shape[2]
    past_len = n_pages * page
    tm = 512
    attn_blk = 256
    attn_kblk = 512
    attn_groups = 4
    attn_rchunk = 32
    ff_chunk = 1024
    sample_nb = 2

    cos_p, sin_p = _rope_tables(np.arange(seq))
    cos_s, sin_s = _rope_tables(past_len + np.arange(dec_batch * dec_seq) % dec_seq)
    cache_krt = jnp.swapaxes(cache_k_rope, 2, 3)

    w_abs, w_o = _prep(w_uq, w_uk, w_uv, w_out)
    w_out_g = w_out[:, N_HEADS * V_DIM:].astype(BF16)
    w_up = w_ff_up.astype(BF16)
    w_down = w_ff_down.astype(BF16)

    xp = x_prompt.reshape(batch * seq, d_model)
    xs = x_sample.reshape(dec_batch * dec_seq, d_model)
    ckv_p, kpe_p, ckv_s, kpe_s, v_s = [], [], [], [], []
    row = lambda a: a.reshape(1, -1)
    for l in range(depth):
        wi = w_in[l]
        kpe_w = wi[:, Q_LORA + KV_LORA:Q_LORA + KV_LORA + ROPE_DIM]
        pad = jnp.zeros((d_model, LANES - 2 * ROPE_DIM), F32)
        w_in_r = jnp.concatenate(
            [wi[:, :Q_LORA + KV_LORA], wi[:, Q_LORA + KV_LORA + ROPE_DIM:], kpe_w,
             _swap_halves(kpe_w), pad], axis=1).astype(BF16)
        wq_pe = w_uq[l][:, :, NOPE_DIM:]
        w_q = jnp.concatenate(
            [w_abs[l], wq_pe.reshape(Q_LORA, QPE_W).astype(BF16),
             _swap_halves(wq_pe).reshape(Q_LORA, QPE_W).astype(BF16)], axis=1)
        w = dict(nm_pre=row(norm_mix_pre[l]), w_in=w_in_r, q_norm=row(q_norm[l]),
                 kv_norm=row(kv_norm[l]), v_norm=row(v_norm[l]), w_q=w_q,
                 w_o=w_o, w_out_g=w_out_g,
                 nm_post=row(norm_mix_post[l]), nf_pre=row(norm_ffn_pre[l]),
                 nf_post=row(norm_ffn_post[l]), w_up=w_up, w_down=w_down)
        ws, bs = w_spatial[l], b_spatial[l]

        gp = _pair_groups(ws)
        gb = jnp.repeat(bs.T, GM_HEAD_DIM, axis=1)
        qst, kcat, ckv, kpe, gout = _proj(xp, w, cos_p, sin_p, gp, gb, period=CHUNK, emit_v=False,
                                          tm=tm, qblk=attn_blk, stack_q=True)
        ckv_p.append(ckv.reshape(batch, seq // page, page, KV_LORA))
        kpe_p.append(kpe.reshape(batch, seq // page, page, ROPE_DIM))
        reps = CHUNK // dec_seq
        gp = _pair_groups(jnp.tile(jnp.tile(ws[:, :dec_seq, :dec_seq], (1, 1, reps)), (1, reps, 1)))
        gb = jnp.tile(jnp.repeat(bs[:, :dec_seq].T, GM_HEAD_DIM, axis=1), (reps, 1))
        q_s, kcat_s, ckv, kpe, gout_s, vn = _proj(xs, w, cos_s, sin_s, gp, gb, period=dec_seq,
                                                  emit_v=True, tm=dec_batch * dec_seq, qblk=attn_blk,
                                                  stack_q=False)
        heads_to_rows = lambda a, wd: a.reshape(dec_batch, dec_seq, N_HEADS, wd).transpose(
            0, 2, 1, 3).reshape(dec_batch, N_HEADS * dec_seq, wd)
        qs = jnp.concatenate([heads_to_rows(q_s[:, :QABS_W], KV_LORA),
                              heads_to_rows(q_s[:, QABS_W:], ROPE_DIM)], axis=-1)
        knew = jnp.pad(kcat_s.reshape(dec_batch, dec_seq, KCAT), ((0, 0), (0, LANES - dec_seq), (0, 0)))

        ol = _attn_prompt(qst, kcat, batch=batch, seq=seq, blk=attn_blk, kblk=attn_kblk,
                          groups=attn_groups, rchunk=attn_rchunk)
        xp = _post(xp, ol, gout, w, layer=l, tm=tm, ff_chunk=ff_chunk)
        o = _attn_sample(page_table, qs, knew, cache_kv_latent, cache_krt, layer=l, nb=sample_nb)
        ol_s = o.reshape(dec_batch, N_HEADS, dec_seq, KV_LORA).transpose(0, 2, 1, 3).reshape(
            dec_batch * dec_seq, QABS_W)
        xs = _post(xs, ol_s, gout_s, w, layer=l, tm=dec_batch * dec_seq, ff_chunk=ff_chunk)
        ckv_s.append(ckv.reshape(dec_batch, dec_seq, KV_LORA))
        kpe_s.append(kpe.reshape(dec_batch, dec_seq, ROPE_DIM))
        v_s.append(vn.reshape(dec_batch, dec_seq, GM_WIDTH))

    return (xp.reshape(batch, seq, d_model), xs.reshape(dec_batch, dec_seq, d_model),
            jnp.stack(ckv_p), jnp.stack(kpe_p), jnp.stack(ckv_s), jnp.stack(kpe_s), jnp.stack(v_s))
```

```python
import functools

import jax
import jax.numpy as jnp
import numpy as np
from jax import lax
from jax.experimental import pallas as pl
from jax.experimental.pallas import tpu as pltpu

N_HEADS = 8
NOPE_DIM = 64
ROPE_DIM = 32
V_DIM = 64
Q_LORA = 256
KV_LORA = 128
GM_GROUPS = 8
GM_HEAD_DIM = 64
GM_WIDTH = GM_GROUPS * GM_HEAD_DIM
CHUNK = 128
ROPE_THETA = 10000.0
EPS = 1e-6
ATTN_SCALE = (NOPE_DIM + ROPE_DIM) ** -0.5
LOG2E = 1.4426950408889634
Q_SCALE = ATTN_SCALE * LOG2E
NEG_INF = -1e30

LANES = 128
KCAT = KV_LORA + ROPE_DIM
QABS_W = N_HEADS * KV_LORA
QPE_W = N_HEADS * ROPE_DIM

C_QLAT = 0
C_CKV = Q_LORA
C_U = C_CKV + KV_LORA
C_V = C_U + GM_WIDTH
C_KPE = C_V + GM_WIDTH
C_KPE_SW = C_KPE + ROPE_DIM
IN_W = C_KPE + LANES

VMEM_LIMIT = 56 * 1024 * 1024

F32 = jnp.float32
BF16 = jnp.bfloat16


def _rms(x, g):
    ms = jnp.mean(x * x, axis=-1, keepdims=True)
    return x * lax.rsqrt(ms + EPS) * g


def _dot(a, b):
    return jnp.dot(a, b, preferred_element_type=F32)


def _lane_tile(x, n):
    return jnp.concatenate([x] * n, axis=1)


def _dot_nt(a, b):
    return lax.dot_general(a, b, (((1,), (1,)), ((), ())), preferred_element_type=F32)


def _prep_kernel(wqn_ref, wuk_ref, wuv_ref, wouta_ref, wabs_ref, wo_ref):
    for h in range(N_HEADS):
        wabs = _dot(wqn_ref[0, h], wuk_ref[0, h]) * Q_SCALE
        wabs_ref[0, :, h * KV_LORA:(h + 1) * KV_LORA] = wabs.astype(BF16)
        wo = _dot(wuv_ref[0, h], wouta_ref[0, h])
        wo_ref[0, h * KV_LORA:(h + 1) * KV_LORA, :] = wo.astype(BF16)


def _prep(w_uq, w_uk, w_uv, w_out):
    depth, _, _, _ = w_uq.shape
    d_model = w_out.shape[-1]
    wqn = jnp.transpose(w_uq[..., :NOPE_DIM], (0, 2, 1, 3))
    wuk = jnp.transpose(w_uk, (0, 2, 3, 1))
    wuv = jnp.transpose(w_uv, (0, 2, 1, 3))
    wouta = w_out[:, :N_HEADS * V_DIM].reshape(depth, N_HEADS, V_DIM, d_model)
    blk = lambda shp: pl.BlockSpec((1,) + shp, lambda l: (l,) + (0,) * len(shp))
    return pl.pallas_call(
        _prep_kernel,
        grid=(depth,),
        in_specs=[blk((N_HEADS, Q_LORA, NOPE_DIM)), blk((N_HEADS, NOPE_DIM, KV_LORA)),
                  blk((N_HEADS, KV_LORA, V_DIM)), blk((N_HEADS, V_DIM, d_model))],
        out_specs=[blk((Q_LORA, QABS_W)), blk((QABS_W, d_model))],
        out_shape=[jax.ShapeDtypeStruct((depth, Q_LORA, QABS_W), BF16),
                   jax.ShapeDtypeStruct((depth, QABS_W, d_model), BF16)],
        name="prep_fold",
    )(wqn, wuk, wuv, wouta)


def _proj_kernel(x_ref, nm_ref, win_ref, qn_ref, kvn_ref, vnw_ref, wq_ref, cos_ref, sin_ref,
                 gp_ref, gb_ref,
                 qst_ref, kcat_ref, ckv_ref, kpe_ref, gout_ref, *vn_out, period, qblk, sub, stack_q):
    tm = x_ref.shape[0]
    row = lax.broadcasted_iota(jnp.int32, (CHUNK, 2 * CHUNK), 0)
    col = lax.broadcasted_iota(jnp.int32, (CHUNK, 2 * CHUNK), 1) & (CHUNK - 1)
    mask = col <= row
    if period < CHUNK:
        sh = period.bit_length() - 1
        mask = mask & ((row >> sh) == (col >> sh))
    left = lax.broadcasted_iota(jnp.int32, (CHUNK, LANES), 1) < GM_HEAD_DIM
    gpk = [jnp.where(mask, gp_ref[k], 0.0).astype(BF16) for k in range(GM_GROUPS // 2)]

    def rows_block(r0, nrows):
        ts = slice(r0, r0 + nrows)
        x = x_ref[ts, :]
        r_x = lax.rsqrt(jnp.mean(x * x, axis=-1, keepdims=True) + EPS)
        zr = _dot((x * nm_ref[...]).astype(BF16), win_ref[...])

        zq = zr[:, C_QLAT:C_QLAT + Q_LORA]
        r_q = lax.rsqrt(r_x * r_x * jnp.mean(zq * zq, axis=-1, keepdims=True) + EPS)
        q = _dot((zq * qn_ref[...]).astype(BF16), wq_ref[...]) * (r_x * r_q)
        z = zr * r_x
        cos = cos_ref[ts, :]
        sin = sin_ref[ts, :]
        qpe = (q[:, QABS_W:QABS_W + QPE_W] * cos + q[:, QABS_W + QPE_W:] * sin) * Q_SCALE
        if not stack_q:
            qst_ref[ts, :QABS_W] = q[:, :QABS_W].astype(BF16)
            qst_ref[ts, QABS_W:] = qpe.astype(BF16)
        for b in range(nrows // qblk if stack_q else 0):
            rs = slice(b * qblk, (b + 1) * qblk)
            for hd in range(N_HEADS):
                hs = slice(hd * qblk, (hd + 1) * qblk)
                qb = r0 // qblk + b
                qst_ref[qb, hs, :KV_LORA] = q[rs, hd * KV_LORA:(hd + 1) * KV_LORA].astype(BF16)
                qst_ref[qb, hs, KV_LORA:] = qpe[rs, hd * ROPE_DIM:(hd + 1) * ROPE_DIM].astype(BF16)

        ckv = _rms(z[:, C_CKV:C_CKV + KV_LORA], kvn_ref[...])
        ckv_ref[ts, :] = ckv
        kpe = (z[:, C_KPE:C_KPE + ROPE_DIM] * cos[:, :ROPE_DIM]
               + z[:, C_KPE_SW:C_KPE_SW + ROPE_DIM] * sin[:, :ROPE_DIM])
        kpe_ref[ts, :] = kpe
        kcat_ref[ts, :KV_LORA] = ckv.astype(BF16)
        kcat_ref[ts, KV_LORA:] = kpe.astype(BF16)

        vn = _rms(z[:, C_V:C_V + GM_WIDTH], vnw_ref[...])
        if vn_out:
            vn_out[0][ts, :] = vn
        for k in range(GM_GROUPS // 2):
            cs = slice(k * LANES, (k + 1) * LANES)
            for c in range(nrows // CHUNK):
                rs = slice(c * CHUNK, (c + 1) * CHUNK)
                vcol = vn[rs, cs]
                rhs = jnp.concatenate([jnp.where(left, vcol, 0.0), jnp.where(left, 0.0, vcol)],
                                      axis=0).astype(BF16)
                mixed = _dot(gpk[k], rhs) + gb_ref[:, cs]
                gout_ref[r0 + c * CHUNK:r0 + (c + 1) * CHUNK, cs] = (
                    z[rs, C_U + k * LANES:C_U + (k + 1) * LANES] * mixed).astype(BF16)

    for r0 in range(0, tm, sub):
        rows_block(r0, sub)


def _proj(x, w, cos, sin, gp, gb, *, period, emit_v, tm, qblk, stack_q):
    n, d = x.shape
    n_pos_tiles = cos.shape[0] // tm
    const = lambda a: pl.BlockSpec(a.shape, lambda i: (0,) * a.ndim)
    rows = lambda wdt: pl.BlockSpec((tm, wdt), lambda i: (i, 0))
    pos = pl.BlockSpec((tm, QPE_W), lambda i: (i % n_pos_tiles, 0))
    out_w = [(KCAT, BF16), (KV_LORA, F32), (ROPE_DIM, F32), (GM_WIDTH, BF16)]
    if emit_v:
        out_w.append((GM_WIDTH, F32))
    ins = [x, w["nm_pre"], w["w_in"], w["q_norm"], w["kv_norm"], w["v_norm"], w["w_q"]]
    qst_rows = N_HEADS * qblk
    if stack_q:
        q_spec = pl.BlockSpec((tm // qblk, qst_rows, KCAT), lambda i: (i, 0, 0))
        q_shape = jax.ShapeDtypeStruct((n // qblk, qst_rows, KCAT), BF16)
    else:
        q_spec = rows(QABS_W + QPE_W)
        q_shape = jax.ShapeDtypeStruct((n, QABS_W + QPE_W), BF16)
    return pl.pallas_call(
        functools.partial(_proj_kernel, period=period, qblk=qblk, sub=qblk, stack_q=stack_q),
        grid=(n // tm,),
        in_specs=[rows(d)] + [const(a) for a in ins[1:]] + [pos, pos, const(gp), const(gb)],
        out_specs=[q_spec] + [rows(wd) for wd, _ in out_w],
        out_shape=[q_shape] + [jax.ShapeDtypeStruct((n, wd), dt) for wd, dt in out_w],
        compiler_params=pltpu.CompilerParams(dimension_semantics=("parallel",),
                                             vmem_limit_bytes=VMEM_LIMIT),
        name="proj",
    )(*ins, cos, sin, gp, gb)


def _attn_prompt_kernel(q_ref, k_ref, o_ref, s_scr, p_scr, m_scr, a_scr, acc_scr,
                        *, blk, kblk, groups, rchunk):
    gheads = N_HEADS // groups
    grows = gheads * blk
    qb = pl.program_id(1)
    m_scr[...] = jnp.full(m_scr.shape, NEG_INF, F32)
    acc_scr[...] = jnp.zeros(acc_scr.shape, F32)
    n_full = (qb * blk) // kblk

    def max_rows(r0, s, k0, masked):
        rows = slice(r0, r0 + rchunk)
        width = s.shape[1]
        if masked:
            qpos = qb * blk + ((r0 + lax.broadcasted_iota(jnp.int32, s.shape, 0)) & (blk - 1))
            kpos = k0 + lax.broadcasted_iota(jnp.int32, s.shape, 1)
            s = jnp.where(kpos <= qpos, s, NEG_INF)
        s_scr[rows, :width] = s
        m_old = m_scr[rows, :]
        m_new = jnp.maximum(m_old, jnp.max(s, axis=-1, keepdims=True))
        a_scr[rows, :] = jnp.exp2(m_old - m_new)
        m_scr[rows, :] = m_new

    def prob_rows(r0, width):
        rows = slice(r0, r0 + rchunk)
        x = s_scr[rows, :width] - _lane_tile(m_scr[rows, :], width // LANES)
        p_scr[rows, :width] = jnp.exp2(x.astype(BF16))

    def key_block(k0, width, masked):
        kt = k_ref[pl.ds(pl.multiple_of(k0, width), width), :]
        vext = jnp.concatenate([kt[:, :KV_LORA], jnp.ones((width, LANES), BF16)], axis=1)
        for g in range(groups):
            grp = slice(g * grows, (g + 1) * grows)
            s = _dot_nt(q_ref[grp, :], kt)
            for c in range(grows // rchunk):
                max_rows(g * grows + c * rchunk, s[c * rchunk:(c + 1) * rchunk, :], k0, masked)
        for g in range(groups):
            grp = slice(g * grows, (g + 1) * grows)
            for c in range(grows // rchunk):
                prob_rows(g * grows + c * rchunk, width)
            acc_scr[grp, :] = (_lane_tile(a_scr[grp, :], 2) * acc_scr[grp, :]
                               + _dot(p_scr[grp, :width], vext))

    def body(kb, carry):
        key_block(kb * kblk, kblk, False)
        return carry

    lax.fori_loop(0, n_full, body, 0)
    k_tail = n_full * kblk
    aligned = k_tail == qb * blk

    @pl.when(aligned)
    def _():
        key_block(k_tail, blk, True)

    @pl.when(jnp.logical_not(aligned))
    def _():
        key_block(k_tail, kblk, True)

    o = (acc_scr[:, :KV_LORA] / acc_scr[:, KV_LORA:]).astype(BF16)
    for h in range(N_HEADS):
        o_ref[:, h * KV_LORA:(h + 1) * KV_LORA] = o[h * blk:(h + 1) * blk, :]


def _attn_prompt(qst, kcat, *, batch, seq, blk, kblk, groups, rchunk):
    nq = seq // blk
    rows_all = N_HEADS * blk
    return pl.pallas_call(
        functools.partial(_attn_prompt_kernel, blk=blk, kblk=kblk, groups=groups, rchunk=rchunk),
        grid=(batch, nq),
        in_specs=[pl.BlockSpec((None, rows_all, KCAT), lambda b, i: (b * nq + i, 0, 0)),
                  pl.BlockSpec((seq, KCAT), lambda b, i: (b, 0))],
        out_specs=pl.BlockSpec((blk, QABS_W), lambda b, i: (b * nq + i, 0)),
        out_shape=jax.ShapeDtypeStruct((batch * seq, QABS_W), BF16),
        scratch_shapes=[pltpu.VMEM((rows_all, kblk), F32),
                        pltpu.VMEM((rows_all, kblk), BF16),
                        pltpu.VMEM((rows_all, LANES), F32),
                        pltpu.VMEM((rows_all, LANES), F32),
                        pltpu.VMEM((rows_all, KV_LORA + LANES), F32)],
        compiler_params=pltpu.CompilerParams(dimension_semantics=("parallel", "parallel"),
                                             vmem_limit_bytes=VMEM_LIMIT),
        name="attn_prompt",
    )(qst, kcat)


def _sample_fetch(pt_ref, ckv_hbm, krt_hbm, kvbuf, krbuf, sems, *, layer, n_pages, page, nb):
    i = pl.program_id(0)
    nsteps = pl.num_programs(0)
    slot = i % 2

    def page_copies(pg, j, p, sl):
        cols = pl.ds(p * page, page)
        return (pltpu.make_async_copy(ckv_hbm.at[layer, pg], kvbuf.at[sl, j, cols], sems.at[0, sl]),
                pltpu.make_async_copy(krt_hbm.at[layer, pg], krbuf.at[sl, j, :, cols], sems.at[1, sl]))

    def start_all(step, sl):
        for j in range(nb):
            for p in range(n_pages):
                for cp in page_copies(pt_ref[step * nb + j, p], j, p, sl):
                    cp.start(priority=p % 2)

    def wait_all(sl):
        for j in range(nb):
            for p in range(n_pages):
                for cp in page_copies(0, j, p, sl):
                    cp.wait()

    @pl.when(i == 0)
    def _():
        start_all(0, 0)

    @pl.when(i + 1 < nsteps)
    def _():
        start_all(i + 1, 1 - slot)

    wait_all(slot)
    return slot


def _sample_stages(q_ref, knew_ref, o_ref, kvbuf, krbuf, slot, *, dec_seq, nb):
    rows = q_ref.shape[1]
    tpos = lax.broadcasted_iota(jnp.int32, (rows, LANES), 0) & (dec_seq - 1)
    jpos = lax.broadcasted_iota(jnp.int32, (rows, LANES), 1)
    new_visible = jpos <= tpos
    kvs, s_pasts, s_news = [], [], []
    for j in range(nb):
        q = q_ref[j]
        kv = kvbuf[slot, j].astype(BF16)
        krt = krbuf[slot, j].astype(BF16)
        kvs.append(kv)
        s_pasts.append(_dot_nt(q[:, :KV_LORA], kv) + _dot(q[:, KV_LORA:], krt))
        s_news.append(jnp.where(new_visible, _dot_nt(q, knew_ref[j]), NEG_INF))
    yield
    probs = []
    for j in range(nb):
        m = jnp.maximum(jnp.max(s_pasts[j], axis=-1, keepdims=True),
                        jnp.max(s_news[j], axis=-1, keepdims=True))
        p_past = jnp.exp2(s_pasts[j] - m)
        p_new = jnp.exp2(s_news[j] - m)
        l = jnp.sum(p_past, axis=-1, keepdims=True) + jnp.sum(p_new, axis=-1, keepdims=True)
        probs.append((p_past.astype(BF16), p_new.astype(BF16), l))
    yield
    for j in range(nb):
        p_past, p_new, l = probs[j]
        acc = _dot(p_past, kvs[j]) + _dot(p_new, knew_ref[j][:, :KV_LORA])
        o_ref[j] = (acc / l).astype(BF16)


def _attn_sample_kernel(pt_ref, q_ref, knew_ref, ckv_hbm, krt_hbm, o_ref, kvbuf, krbuf, sems,
                        *, layer, n_pages, page, dec_seq, nb):
    slot = _sample_fetch(pt_ref, ckv_hbm, krt_hbm, kvbuf, krbuf, sems, layer=layer, n_pages=n_pages,
                         page=page, nb=nb)
    for _ in _sample_stages(q_ref, knew_ref, o_ref, kvbuf, krbuf, slot, dec_seq=dec_seq, nb=nb):
        pass


def _attn_sample(page_table, qs, knew, cache_kv, cache_krt, *, layer, nb):
    db, rows, _ = qs.shape
    n_pages = page_table.shape[1]
    page = cache_kv.shape[2]
    dec_seq = rows // N_HEADS
    grid_spec = pltpu.PrefetchScalarGridSpec(
        num_scalar_prefetch=1,
        grid=(db // nb,),
        in_specs=[pl.BlockSpec((nb, rows, KCAT), lambda i, pt: (i, 0, 0)),
                  pl.BlockSpec((nb, LANES, KCAT), lambda i, pt: (i, 0, 0)),
                  pl.BlockSpec(memory_space=pl.ANY),
                  pl.BlockSpec(memory_space=pl.ANY)],
        out_specs=pl.BlockSpec((nb, rows, KV_LORA), lambda i, pt: (i, 0, 0)),
        scratch_shapes=[pltpu.VMEM((2, nb, n_pages * page, KV_LORA), F32),
                        pltpu.VMEM((2, nb, ROPE_DIM, n_pages * page), F32),
                        pltpu.SemaphoreType.DMA((2, 2))],
    )
    return pl.pallas_call(
        functools.partial(_attn_sample_kernel, layer=layer, n_pages=n_pages, page=page,
                          dec_seq=dec_seq, nb=nb),
        grid_spec=grid_spec,
        out_shape=jax.ShapeDtypeStruct((db, rows, KV_LORA), BF16),
        compiler_params=pltpu.CompilerParams(dimension_semantics=("arbitrary",),
                                             vmem_limit_bytes=VMEM_LIMIT),
        name="attn_sample",
    )(page_table, qs, knew, cache_kv, cache_krt)


def _post_kernel(x_ref, ol_ref, g_ref, wo_ref, wg_ref, nmp_ref, nfp_ref, nfo_ref, wup_ref, wdn_ref,
                 y_ref, *, ff_chunk):
    mix = _dot(ol_ref[...], wo_ref[...]) + _dot(g_ref[...], wg_ref[...])
    x1 = x_ref[...] + _rms(mix, nmp_ref[...])
    h2 = _rms(x1, nfp_ref[...]).astype(BF16)
    d_ff = wup_ref.shape[1]
    f = None
    for c in range(d_ff // ff_chunk):
        cs = slice(c * ff_chunk, (c + 1) * ff_chunk)
        hf = _dot(h2, wup_ref[:, cs])
        act = jnp.square(jnp.maximum(hf, 0.0)).astype(BF16)
        part = _dot(act, wdn_ref[cs, :])
        f = part if f is None else f + part
    y_ref[...] = x1 + _rms(f, nfo_ref[...])


def _post(x, ol, gout, w, *, layer, tm, ff_chunk):
    n, d = x.shape

    def const(a):
        if a.ndim == 3:
            return pl.BlockSpec((None,) + a.shape[1:], lambda i: (layer, 0, 0),
                                pipeline_mode=pl.Buffered(1))
        return pl.BlockSpec(a.shape, lambda i: (0, 0), pipeline_mode=pl.Buffered(1))

    rows = lambda wdt: pl.BlockSpec((tm, wdt), lambda i: (i, 0))
    ws = [w["w_o"], w["w_out_g"], w["nm_post"], w["nf_pre"], w["nf_post"], w["w_up"], w["w_down"]]
    return pl.pallas_call(
        functools.partial(_post_kernel, ff_chunk=ff_chunk),
        grid=(n // tm,),
        in_specs=[rows(d), rows(QABS_W), rows(GM_WIDTH)] + [const(a) for a in ws],
        out_specs=rows(d),
        out_shape=jax.ShapeDtypeStruct((n, d), F32),
        compiler_params=pltpu.CompilerParams(dimension_semantics=("parallel",),
                                             vmem_limit_bytes=VMEM_LIMIT),
        name="post",
    )(x, ol, gout, *ws)


def _rope_tables(pos):
    half = ROPE_DIM // 2
    freqs = ROPE_THETA ** (-np.arange(half, dtype=np.float64) / half)
    ang = pos.astype(np.float64)[:, None] * freqs[None, :]
    tile = lambda t: jnp.asarray(np.tile(t.astype(np.float32), (1, 2 * N_HEADS)))
    return tile(np.cos(ang)), tile(np.sin(ang))


def _swap_halves(w):
    half = w.shape[-1] // 2
    return jnp.concatenate([-w[..., half:], w[..., :half]], axis=-1)


def _pair_groups(g):
    n = g.shape[0]
    return g.reshape(n // 2, 2, CHUNK, CHUNK).transpose(0, 2, 1, 3).reshape(n // 2, CHUNK, 2 * CHUNK)


def kernel(x_prompt, x_sample, cache_kv_latent, cache_k_rope, page_table, norm_mix_pre, norm_mix_post,
           w_in, q_norm, w_uq, kv_norm, w_uk, w_uv, v_norm, w_spatial, b_spatial, w_out, norm_ffn_pre,
           norm_ffn_post, w_ff_up, w_ff_down):
    batch, seq, d_model = x_prompt.shape
    dec_batch, dec_seq, _ = x_sample.shape
    depth = w_in.shape[0]
    n_pages, page = page_table.shape[1], cache_kv_latent.shape[2]
    past_len = n_pages * page
    tm = 512
    attn_blk = 256
    attn_kblk = 512
    attn_groups = 4
    attn_rchunk = 32
    ff_chunk = 1024
    post_tm, post_ff_chunk = 1024, 512
    sample_nb = 2

    cos_p, sin_p = _rope_tables(np.arange(seq))
    cos_s, sin_s = _rope_tables(past_len + np.arange(dec_batch * dec_seq) % dec_seq)
    cache_krt = jnp.swapaxes(cache_k_rope, 2, 3)

    w_abs, w_o = _prep(w_uq, w_uk, w_uv, w_out)
    w_out_g = w_out[:, N_HEADS * V_DIM:].astype(BF16)
    w_up = w_ff_up.astype(BF16)
    w_down = w_ff_down.astype(BF16)

    xp = x_prompt.reshape(batch * seq, d_model)
    xs = x_sample.reshape(dec_batch * dec_seq, d_model)
    ckv_p, kpe_p, ckv_s, kpe_s, v_s = [], [], [], [], []
    row = lambda a: a.reshape(1, -1)
    for l in range(depth):
        wi = w_in[l]
        kpe_w = wi[:, Q_LORA + KV_LORA:Q_LORA + KV_LORA + ROPE_DIM]
        pad = jnp.zeros((d_model, LANES - 2 * ROPE_DIM), F32)
        w_in_r = jnp.concatenate(
            [wi[:, :Q_LORA + KV_LORA], wi[:, Q_LORA + KV_LORA + ROPE_DIM:], kpe_w,
             _swap_halves(kpe_w), pad], axis=1).astype(BF16)
        wq_pe = w_uq[l][:, :, NOPE_DIM:]
        w_q = jnp.concatenate(
            [w_abs[l], wq_pe.reshape(Q_LORA, QPE_W).astype(BF16),
             _swap_halves(wq_pe).reshape(Q_LORA, QPE_W).astype(BF16)], axis=1)
        w = dict(nm_pre=row(norm_mix_pre[l]), w_in=w_in_r, q_norm=row(q_norm[l]),
                 kv_norm=row(kv_norm[l]), v_norm=row(v_norm[l]), w_q=w_q,
                 w_o=w_o, w_out_g=w_out_g,
                 nm_post=row(norm_mix_post[l]), nf_pre=row(norm_ffn_pre[l]),
                 nf_post=row(norm_ffn_post[l]), w_up=w_up, w_down=w_down)
        ws, bs = w_spatial[l], b_spatial[l]

        gp = _pair_groups(ws)
        gb = jnp.repeat(bs.T, GM_HEAD_DIM, axis=1)
        qst, kcat, ckv, kpe, gout = _proj(xp, w, cos_p, sin_p, gp, gb, period=CHUNK, emit_v=False,
                                          tm=tm, qblk=attn_blk, stack_q=True)
        ckv_p.append(ckv.reshape(batch, seq // page, page, KV_LORA))
        kpe_p.append(kpe.reshape(batch, seq // page, page, ROPE_DIM))
        reps = CHUNK // dec_seq
        gp = _pair_groups(jnp.tile(jnp.tile(ws[:, :dec_seq, :dec_seq], (1, 1, reps)), (1, reps, 1)))
        gb = jnp.tile(jnp.repeat(bs[:, :dec_seq].T, GM_HEAD_DIM, axis=1), (reps, 1))
        q_s, kcat_s, ckv, kpe, gout_s, vn = _proj(xs, w, cos_s, sin_s, gp, gb, period=dec_seq,
                                                  emit_v=True, tm=dec_batch * dec_seq, qblk=attn_blk,
                                                  stack_q=False)
        heads_to_rows = lambda a, wd: a.reshape(dec_batch, dec_seq, N_HEADS, wd).transpose(
            0, 2, 1, 3).reshape(dec_batch, N_HEADS * dec_seq, wd)
        qs = jnp.concatenate([heads_to_rows(q_s[:, :QABS_W], KV_LORA),
                              heads_to_rows(q_s[:, QABS_W:], ROPE_DIM)], axis=-1)
        knew = jnp.pad(kcat_s.reshape(dec_batch, dec_seq, KCAT), ((0, 0), (0, LANES - dec_seq), (0, 0)))

        ol = _attn_prompt(qst, kcat, batch=batch, seq=seq, blk=attn_blk, kblk=attn_kblk,
                          groups=attn_groups, rchunk=attn_rchunk)
        xp = _post(xp, ol, gout, w, layer=l, tm=post_tm, ff_chunk=post_ff_chunk)
        o = _attn_sample(page_table, qs, knew, cache_kv_latent, cache_krt, layer=l, nb=sample_nb)
        ol_s = o.reshape(dec_batch, N_HEADS, dec_seq, KV_LORA).transpose(0, 2, 1, 3).reshape(
            dec_batch * dec_seq, QABS_W)
        xs = _post(xs, ol_s, gout_s, w, layer=l, tm=dec_batch * dec_seq, ff_chunk=ff_chunk)
        ckv_s.append(ckv.reshape(dec_batch, dec_seq, KV_LORA))
        kpe_s.append(kpe.reshape(dec_batch, dec_seq, ROPE_DIM))
        v_s.append(vn.reshape(dec_batch, dec_seq, GM_WIDTH))

    return (xp.reshape(batch, seq, d_model), xs.reshape(dec_batch, dec_seq, d_model),
            jnp.stack(ckv_p), jnp.stack(kpe_p), jnp.stack(ckv_s), jnp.stack(kpe_s), jnp.stack(v_s))
```
